```python
import math
import jax, jax.numpy as jnp
from jax import lax
import numpy as np

D_MODEL = 1024
BATCH = 8
SEQ = 4096
DEPTH = 4

CHUNK = 64
Q_BLOCK = 128
N_EVEN = (DEPTH + 1) // 2
N_ODD = DEPTH // 2
DIFF_HEADS = 4
DIFF_HEAD_DIM = 64
DIFF_V_DIM = 2 * DIFF_HEAD_DIM
DIFF_WIDTH = DIFF_HEADS * DIFF_V_DIM
LRU_WIDTH = 512
LRU_BLOCKS = 8
LRU_BLOCK_DIM = LRU_WIDTH // LRU_BLOCKS
LRU_C = 8.0
CONV_WIDTH = 4
AB_IN = 3 * DIFF_WIDTH + 2 * LRU_WIDTH
AB_OUT = DIFF_WIDTH + LRU_WIDTH
SGU_WIDTH = D_MODEL
SGU_GROUPS = 8
SGU_GROUP_DIM = SGU_WIDTH // SGU_GROUPS
SGU_BLOCK = 128
D_FF = 4 * D_MODEL
EPS = 1e-6

kernel_name = 'hybrid_diffattn_rglru_gmlp_trunk'


def rms_norm(x, g):
    xf = x.astype(jnp.float32)
    y = xf * lax.rsqrt(jnp.mean(xf * xf, axis=-1, keepdims=True) + EPS)
    return (y * g.astype(jnp.float32)).astype(x.dtype)


def layer_norm(x, g, b):
    xf = x.astype(jnp.float32)
    mu = jnp.mean(xf, axis=-1, keepdims=True)
    var = jnp.mean(jnp.square(xf - mu), axis=-1, keepdims=True)
    y = (xf - mu) * lax.rsqrt(var + EPS)
    return (y * g.astype(jnp.float32) + b.astype(jnp.float32)).astype(x.dtype)


def diff_attention(q, k, v, lam, lam_init, subln_g):
    b, s, _ = q.shape
    nb = s // Q_BLOCK
    qb = q.reshape(b, nb, Q_BLOCK, DIFF_HEADS, 2, DIFF_HEAD_DIM).swapaxes(0, 1)
    k = k.reshape(b, s, DIFF_HEADS, 2, DIFF_HEAD_DIM)
    v = v.reshape(b, s, DIFF_HEADS, DIFF_V_DIM)
    scale = DIFF_HEAD_DIM ** -0.5
    k_chunk = jnp.arange(s) // CHUNK

    def block(args):
        q_blk, i = args
        scores = jnp.einsum('bqhmd,bkhmd->bhmqk', q_blk, k).astype(jnp.float32) * scale
        q_chunk = (i * Q_BLOCK + jnp.arange(Q_BLOCK)) // CHUNK
        mask = k_chunk[None, :] <= q_chunk[:, None]
        probs = jax.nn.softmax(jnp.where(mask, scores, -jnp.inf), axis=-1)
        w = probs[:, :, 0] - lam * probs[:, :, 1]
        return jnp.einsum('bhqk,bkhv->bqhv', w.astype(v.dtype), v)

    out = lax.map(block, (qb, jnp.arange(nb)))
    out = out.swapaxes(0, 1).reshape(b, s, DIFF_HEADS, DIFF_V_DIM)
    out = rms_norm(out, subln_g) * (1.0 - lam_init)
    return out.reshape(b, s, DIFF_WIDTH)


def rg_lru_branch(xb, gate, conv_w, conv_b, wa, ba, wx, bx, lam):
    c = xb.shape[-1]
    xc = lax.conv_general_dilated(
        xb, conv_w[:, None, :].astype(xb.dtype), window_strides=(1,),
        padding=[(CONV_WIDTH - 1, 0)], dimension_numbers=('NWC', 'WIO', 'NWC'),
        feature_group_count=c) + conv_b
    b, s, _ = xc.shape
    xh = xc.reshape(b, s, LRU_BLOCKS, LRU_BLOCK_DIM)
    r = jax.nn.sigmoid(jnp.einsum('bshi,hij->bshj', xh, wa).reshape(b, s, c) + ba)
    i = jax.nn.sigmoid(jnp.einsum('bshi,hij->bshj', xh, wx).reshape(b, s, c) + bx)
    log_a = -LRU_C * r.astype(jnp.float32) * jax.nn.softplus(-lam.astype(jnp.float32))
    a = jnp.exp(log_a)
    bt = jnp.sqrt(-jnp.expm1(2.0 * log_a)) * (i * xc).astype(jnp.float32)

    def combine(left, right):
        a1, b1 = left
        a2, b2 = right
        return a1 * a2, a2 * b1 + b2

    _, h = lax.associative_scan(combine, (a, bt), axis=1)
    return jax.nn.gelu(gate) * h.astype(xb.dtype)


def spatial_gating(z, ln_g, ln_b, w_s, b_s):
    u, v = jnp.split(z, 2, axis=-1)
    v = layer_norm(v, ln_g, ln_b)
    b, s, _ = v.shape
    vb = v.reshape(b, s // SGU_BLOCK, SGU_BLOCK, SGU_GROUPS, SGU_GROUP_DIM)
    pos_chunk = jnp.arange(SGU_BLOCK) // CHUNK
    mask = pos_chunk[None, :] <= pos_chunk[:, None]
    w = jnp.where(mask[None], w_s, 0)
    mixed = jnp.einsum('gpq,bnqgc->bnpgc', w, vb) + b_s.T[None, None, :, :, None]
    return u * mixed.reshape(b, s, SGU_WIDTH)


def setup_inputs(seed: int = 0) -> dict:
    key = jax.random.key(seed)
    ks = iter(jax.random.split(key, 40))

    def nrm(shape, scale):
        return jax.random.normal(next(ks), shape, jnp.float32) * scale

    x = nrm((BATCH, SEQ, D_MODEL), 1.0)
    norm_mix = 1.0 + nrm((DEPTH, D_MODEL), 0.02)
    norm_ffn = 1.0 + nrm((DEPTH, D_MODEL), 0.02)
    norm_final = 1.0 + nrm((D_MODEL,), 0.02)
    ab_w_in = nrm((N_EVEN, D_MODEL, AB_IN), D_MODEL ** -0.5)
    diff_lq1 = nrm((N_EVEN, DIFF_HEAD_DIM), 0.1)
    diff_lk1 = nrm((N_EVEN, DIFF_HEAD_DIM), 0.1)
    diff_lq2 = nrm((N_EVEN, DIFF_HEAD_DIM), 0.1)
    diff_lk2 = nrm((N_EVEN, DIFF_HEAD_DIM), 0.1)
    diff_subln = 1.0 + nrm((N_EVEN, DIFF_V_DIM), 0.02)
    lru_conv_w = nrm((N_EVEN, CONV_WIDTH, LRU_WIDTH), CONV_WIDTH ** -0.5)
    lru_conv_b = nrm((N_EVEN, LRU_WIDTH), 0.01)
    lru_wa = nrm((N_EVEN, LRU_BLOCKS, LRU_BLOCK_DIM, LRU_BLOCK_DIM), LRU_BLOCK_DIM ** -0.5)
    lru_ba = nrm((N_EVEN, LRU_WIDTH), 0.01)
    lru_wx = nrm((N_EVEN, LRU_BLOCKS, LRU_BLOCK_DIM, LRU_BLOCK_DIM), LRU_BLOCK_DIM ** -0.5)
    lru_bx = nrm((N_EVEN, LRU_WIDTH), 0.01)
    a0 = jax.random.uniform(next(ks), (N_EVEN, LRU_WIDTH), jnp.float32, 0.9, 0.999)
    s0 = a0 ** (1.0 / LRU_C)
    lru_lambda = jnp.log(s0) - jnp.log1p(-s0)
    ab_w_out = nrm((N_EVEN, AB_OUT, D_MODEL), AB_OUT ** -0.5)
    c_w_in = nrm((N_ODD, D_MODEL, 2 * SGU_WIDTH), D_MODEL ** -0.5)
    c_ln_g = 1.0 + nrm((N_ODD, SGU_WIDTH), 0.02)
    c_ln_b = nrm((N_ODD, SGU_WIDTH), 0.01)
    c_w_s = nrm((N_ODD, SGU_GROUPS, SGU_BLOCK, SGU_BLOCK), SGU_BLOCK ** -0.5)
    c_b_s = 1.0 + nrm((N_ODD, SGU_GROUPS, SGU_BLOCK), 0.01)
    c_w_out = nrm((N_ODD, SGU_WIDTH, D_MODEL), SGU_WIDTH ** -0.5)
    ffn_w1 = nrm((DEPTH, D_MODEL, D_FF), D_MODEL ** -0.5)
    ffn_w2 = nrm((DEPTH, D_FF, D_MODEL), D_FF ** -0.5)
    return {'x': x, 'norm_mix': norm_mix, 'norm_ffn': norm_ffn, 'norm_final': norm_final,
            'ab_w_in': ab_w_in, 'diff_lq1': diff_lq1, 'diff_lk1': diff_lk1,
            'diff_lq2': diff_lq2, 'diff_lk2': diff_lk2, 'diff_subln': diff_subln,
            'lru_conv_w': lru_conv_w, 'lru_conv_b': lru_conv_b, 'lru_wa': lru_wa,
            'lru_ba': lru_ba, 'lru_wx': lru_wx, 'lru_bx': lru_bx, 'lru_lambda': lru_lambda,
            'ab_w_out': ab_w_out, 'c_w_in': c_w_in, 'c_ln_g': c_ln_g, 'c_ln_b': c_ln_b,
            'c_w_s': c_w_s, 'c_b_s': c_b_s, 'c_w_out': c_w_out,
            'ffn_w1': ffn_w1, 'ffn_w2': ffn_w2}


def reference(x, norm_mix, norm_ffn, norm_final, ab_w_in, diff_lq1, diff_lk1,
              diff_lq2, diff_lk2, diff_subln, lru_conv_w, lru_conv_b, lru_wa,
              lru_ba, lru_wx, lru_bx, lru_lambda, ab_w_out, c_w_in, c_ln_g, c_ln_b,
              c_w_s, c_b_s, c_w_out, ffn_w1, ffn_w2):
    splits = [DIFF_WIDTH, 2 * DIFF_WIDTH, 3 * DIFF_WIDTH, 3 * DIFF_WIDTH + LRU_WIDTH]
    for layer in range(DEPTH):
        h = rms_norm(x, norm_mix[layer])
        if layer % 2 == 0:
            e = layer // 2
            proj = h @ ab_w_in[e]
            q, k, v, xb, gate = jnp.split(proj, splits, axis=-1)
            lam_init = 0.8 - 0.6 * math.exp(-0.3 * layer)
            lam = (jnp.exp(jnp.sum(diff_lq1[e].astype(jnp.float32) * diff_lk1[e].astype(jnp.float32)))
                   - jnp.exp(jnp.sum(diff_lq2[e].astype(jnp.float32) * diff_lk2[e].astype(jnp.float32)))
                   + lam_init)
            ya = diff_attention(q, k, v, lam, lam_init, diff_subln[e])
            yb = rg_lru_branch(xb, gate, lru_conv_w[e], lru_conv_b[e], lru_wa[e], lru_ba[e],
                               lru_wx[e], lru_bx[e], lru_lambda[e])
            mix = jnp.concatenate([ya, yb], axis=-1) @ ab_w_out[e]
        else:
            o = layer // 2
            z = jax.nn.gelu(h @ c_w_in[o])
            mix = spatial_gating(z, c_ln_g[o], c_ln_b[o], c_w_s[o], c_b_s[o]) @ c_w_out[o]
        x = x + mix
        h = rms_norm(x, norm_ffn[layer])
        x = x + jnp.square(jax.nn.relu(h @ ffn_w1[layer])) @ ffn_w2[layer]
    return rms_norm(x, norm_final)
```

```python
import functools
import math

import jax
import jax.numpy as jnp
from jax import lax
from jax.experimental import pallas as pl
from jax.experimental.pallas import tpu as pltpu

D_MODEL = 1024
DEPTH = 4
CHUNK = 64
DIFF_HEADS = 4
DIFF_HEAD_DIM = 64
DIFF_V_DIM = 2 * DIFF_HEAD_DIM
DIFF_WIDTH = DIFF_HEADS * DIFF_V_DIM
LRU_WIDTH = 512
LRU_BLOCKS = 8
LRU_BLOCK_DIM = LRU_WIDTH // LRU_BLOCKS
LRU_C = 8.0
CONV_WIDTH = 4
SGU_WIDTH = D_MODEL
SGU_GROUPS = 8
SGU_GROUP_DIM = SGU_WIDTH // SGU_GROUPS
SGU_BLOCK = 128
D_FF = 4 * D_MODEL
EPS = 1e-6

F32 = jnp.float32
BF16 = jnp.bfloat16

VMEM_LIMIT_BYTES = 56 * 1024 * 1024
SUBLANES = 8

TOKEN_TILE = 512
FF_CHUNK = 1024
Q_TILE = 256
KV_TILE = 256
LRU_TILE = 512
NEG_BIG = -1e30


def _resident(shape):
    return pl.BlockSpec(shape, lambda *_: (0,) * len(shape),
                        pipeline_mode=pl.Buffered(1))


def _rms(x, g):
    return x * lax.rsqrt(jnp.mean(x * x, axis=-1, keepdims=True) + EPS) * g


def _dot(a, b):
    return jnp.dot(a, b, preferred_element_type=F32)


def _dot_nt(a, b):
    return lax.dot_general(a, b, (((1,), (1,)), ((), ())), preferred_element_type=F32)


def _ab_in_kernel(x_ref, g_ref, w_ref, q_ref, k_ref, v_ref, xb_ref, gate_ref):
    h = _rms(x_ref[...], g_ref[...]).astype(BF16)
    w = DIFF_WIDTH
    q_ref[...] = (_dot(h, w_ref[:, 0:w]) * (DIFF_HEAD_DIM ** -0.5)).astype(BF16)
    k_ref[...] = _dot(h, w_ref[:, w:2 * w]).astype(BF16)
    v_ref[...] = _dot(h, w_ref[:, 2 * w:3 * w]).astype(BF16)
    xb_ref[...] = _dot(h, w_ref[:, 3 * w:3 * w + LRU_WIDTH])
    gate_ref[...] = _dot(h, w_ref[:, 3 * w + LRU_WIDTH:])


def _ab_in(x2, g, w_in):
    t = x2.shape[0]
    n_in = w_in.shape[1]
    row = lambda i: (i, 0)
    return pl.pallas_call(
        _ab_in_kernel,
        grid=(t // TOKEN_TILE,),
        in_specs=[pl.BlockSpec((TOKEN_TILE, D_MODEL), row),
                  _resident((1, D_MODEL)),
                  _resident((D_MODEL, n_in))],
        out_specs=[pl.BlockSpec((TOKEN_TILE, DIFF_WIDTH), row)] * 3
        + [pl.BlockSpec((TOKEN_TILE, LRU_WIDTH), row)] * 2,
        out_shape=[jax.ShapeDtypeStruct((t, DIFF_WIDTH), BF16)] * 3
        + [jax.ShapeDtypeStruct((t, LRU_WIDTH), F32)] * 2,
        compiler_params=pltpu.CompilerParams(
            dimension_semantics=("arbitrary",), vmem_limit_bytes=VMEM_LIMIT_BYTES),
        name="ab_in_proj",
    )(x2, g, w_in)


def _attn_kernel(lam_init, q_ref, k_ref, v_ref, lq1_ref, lk1_ref, lq2_ref, lk2_ref,
                 sg_ref, o_ref, m_ref, l_ref, acc_ref):
    i = pl.program_id(2)
    d = DIFF_HEAD_DIM
    q = q_ref[...]
    qs = (q[:, :d], q[:, d:])

    m_ref[...] = jnp.full(m_ref.shape, NEG_BIG, F32)
    l_ref[...] = jnp.zeros(l_ref.shape, F32)
    acc_ref[...] = jnp.zeros(acc_ref.shape, F32)

    def step(j, mask):
        start = pl.multiple_of(j * KV_TILE, KV_TILE)
        kb = k_ref[pl.ds(start, KV_TILE), :]
        vb = v_ref[pl.ds(start, KV_TILE), :]
        for mp in range(2):
            s = _dot_nt(qs[mp], kb[:, mp * d:(mp + 1) * d])
            if mask is not None:
                s = jnp.where(mask, s, NEG_BIG)
            m_prev = m_ref[mp]
            m_new = jnp.maximum(m_prev, jnp.max(s, axis=-1, keepdims=True))
            alpha = jnp.exp(m_prev - m_new)
            p = jnp.exp(s - m_new)
            l_ref[mp] = alpha * l_ref[mp] + jnp.sum(p, axis=-1, keepdims=True)
            acc_ref[mp] = alpha * acc_ref[mp] + _dot(p.astype(BF16), vb)
            m_ref[mp] = m_new

    def body(j, carry):
        step(j, None)
        return carry

    lax.fori_loop(0, i, body, 0)
    qc = lax.broadcasted_iota(jnp.int32, (Q_TILE, KV_TILE), 0) // CHUNK
    kc = lax.broadcasted_iota(jnp.int32, (Q_TILE, KV_TILE), 1) // CHUNK
    step(i, kc <= qc)

    lam = (jnp.exp(jnp.sum(lq1_ref[...] * lk1_ref[...], axis=-1, keepdims=True))
           - jnp.exp(jnp.sum(lq2_ref[...] * lk2_ref[...], axis=-1, keepdims=True))
           + lam_init)
    o = acc_ref[0] / l_ref[0] - lam * (acc_ref[1] / l_ref[1])
    o = _rms(o, sg_ref[...]) * (1.0 - lam_init)
    o_ref[...] = o.astype(o_ref.dtype)


def _diff_attention(q, k, v, lq1, lk1, lq2, lk2, subln, lam_init):
    b, s, _ = q.shape
    assert Q_TILE == KV_TILE and Q_TILE % CHUNK == 0 and s % Q_TILE == 0
    vd = DIFF_V_DIM
    vec = lambda n: _resident((1, n))
    return pl.pallas_call(
        functools.partial(_attn_kernel, lam_init),
        grid=(b, DIFF_HEADS, s // Q_TILE),
        in_specs=[pl.BlockSpec((None, Q_TILE, vd), lambda bi, h, i: (bi, i, h)),
                  pl.BlockSpec((None, s, vd), lambda bi, h, i: (bi, 0, h)),
                  pl.BlockSpec((None, s, vd), lambda bi, h, i: (bi, 0, h)),
                  vec(DIFF_HEAD_DIM), vec(DIFF_HEAD_DIM), vec(DIFF_HEAD_DIM),
                  vec(DIFF_HEAD_DIM), vec(vd)],
        out_specs=pl.BlockSpec((None, Q_TILE, vd), lambda bi, h, i: (bi, i, h)),
        out_shape=jax.ShapeDtypeStruct((b, s, DIFF_WIDTH), BF16),
        scratch_shapes=[pltpu.VMEM((2, Q_TILE, 1), F32),
                        pltpu.VMEM((2, Q_TILE, 1), F32),
                        pltpu.VMEM((2, Q_TILE, vd), F32)],
        compiler_params=pltpu.CompilerParams(
            dimension_semantics=("arbitrary",) * 3, vmem_limit_bytes=VMEM_LIMIT_BYTES),
        name="diff_attention",
    )(q, k, v, lq1, lk1, lq2, lk2, subln)


def _lru_kernel(xb_ref, gate_ref, cw_ref, cb_ref, wg_ref, ba_ref, bx_ref, lam_ref,
                y_ref, xpad_ref, h_ref):
    ts = LRU_TILE
    pad = SUBLANES

    @pl.when(pl.program_id(1) == 0)
    def _():
        xpad_ref[0:pad, :] = jnp.zeros((pad, LRU_WIDTH), F32)
        h_ref[...] = jnp.zeros(h_ref.shape, F32)

    xpad_ref[pad:pad + ts, :] = xb_ref[...]
    xc = cb_ref[...]
    for j in range(CONV_WIDTH):
        off = pad - (CONV_WIDTH - 1) + j
        xc = xc + cw_ref[j:j + 1, :] * xpad_ref[off:off + ts, :]
    xpad_ref[0:pad, :] = xpad_ref[ts:ts + pad, :]

    gates = _dot(xc.astype(BF16), wg_ref[...])
    r = jax.nn.sigmoid(gates[:, :LRU_WIDTH] + ba_ref[...])
    ig = jax.nn.sigmoid(gates[:, LRU_WIDTH:] + bx_ref[...])
    z = -lam_ref[...]
    softplus = jnp.maximum(z, 0.0) + jnp.log1p(jnp.exp(-jnp.abs(z)))
    log_a = -LRU_C * r * softplus
    a = jnp.exp(log_a)
    bt = jnp.sqrt(-jnp.tanh(log_a) * (a * a + 1.0)) * (ig * xc)

    row = lax.broadcasted_iota(jnp.int32, (ts, LRU_WIDTH), 0)
    shift = 1
    while shift < ts:
        keep = row >= shift
        a_prev = jnp.where(keep, pltpu.roll(a, shift, 0), 1.0)
        b_prev = jnp.where(keep, pltpu.roll(bt, shift, 0), 0.0)
        bt = a * b_prev + bt
        a = a * a_prev
        shift *= 2
    h = a * h_ref[...] + bt
    h_ref[...] = h[ts - 1:ts, :]
    y_ref[...] = (jax.nn.gelu(gate_ref[...]) * h).astype(y_ref.dtype)


def _rg_lru(xb, gate, conv_w, conv_b, w_gates, ba, bx, lam):
    b, s, c = xb.shape
    seq = lambda bi, si: (bi, si, 0)
    return pl.pallas_call(
        _lru_kernel,
        grid=(b, s // LRU_TILE),
        in_specs=[pl.BlockSpec((None, LRU_TILE, c), seq),
                  pl.BlockSpec((None, LRU_TILE, c), seq),
                  _resident((CONV_WIDTH, c)), _resident((1, c)),
                  _resident((c, 2 * c)), _resident((1, c)), _resident((1, c)),
                  _resident((1, c))],
        out_specs=pl.BlockSpec((None, LRU_TILE, c), seq),
        out_shape=jax.ShapeDtypeStruct((b, s, c), BF16),
        scratch_shapes=[pltpu.VMEM((LRU_TILE + SUBLANES, c), F32),
                        pltpu.VMEM((1, c), F32)],
        compiler_params=pltpu.CompilerParams(
            dimension_semantics=("arbitrary", "arbitrary"),
            vmem_limit_bytes=VMEM_LIMIT_BYTES),
        name="rg_lru",
    )(xb, gate, conv_w, conv_b, w_gates, ba, bx, lam)


def _block_diag(w):
    nb, d, _ = w.shape
    eye = jnp.eye(nb, dtype=w.dtype)
    return (eye[:, None, :, None] * w[:, :, None, :]).reshape(nb * d, nb * d)


def _ffn(x, g, w1_ref, w2_ref):
    h = _rms(x, g).astype(BF16)
    acc = None
    for c in range(D_FF // FF_CHUNK):
        cols = slice(c * FF_CHUNK, (c + 1) * FF_CHUNK)
        t = jnp.maximum(_dot(h, w1_ref[:, cols]), 0.0)
        part = _dot((t * t).astype(BF16), w2_ref[cols, :])
        acc = part if acc is None else acc + part
    return x + acc


def _finish(x, gfin_ref, o_ref):
    if gfin_ref is not None:
        x = _rms(x, gfin_ref[...])
    o_ref[...] = x


def _ab_out_ffn_kernel(final, x_ref, ya_ref, yb_ref, wo_ref, gf_ref, w1_ref, w2_ref,
                       *rest):
    gfin_ref, o_ref = rest if final else (None, rest[0])
    y = jnp.concatenate([ya_ref[...], yb_ref[...]], axis=1)
    x = x_ref[...] + _dot(y, wo_ref[...])
    _finish(_ffn(x, gf_ref[...], w1_ref, w2_ref), gfin_ref, o_ref)


def _ab_out_ffn(x2, ya, yb, w_out, g_ffn, w1, w2, g_final):
    t = x2.shape[0]
    row = lambda i: (i, 0)
    final = g_final is not None
    in_specs = [pl.BlockSpec((TOKEN_TILE, D_MODEL), row),
                pl.BlockSpec((TOKEN_TILE, DIFF_WIDTH), row),
                pl.BlockSpec((TOKEN_TILE, LRU_WIDTH), row),
                _resident(w_out.shape), _resident((1, D_MODEL)),
                _resident(w1.shape), _resident(w2.shape)]
    args = [x2, ya, yb, w_out, g_ffn, w1, w2]
    if final:
        in_specs.append(_resident((1, D_MODEL)))
        args.append(g_final)
    return pl.pallas_call(
        functools.partial(_ab_out_ffn_kernel, final),
        grid=(t // TOKEN_TILE,),
        in_specs=in_specs,
        out_specs=pl.BlockSpec((TOKEN_TILE, D_MODEL), row),
        out_shape=jax.ShapeDtypeStruct((t, D_MODEL), F32),
        compiler_params=pltpu.CompilerParams(
            dimension_semantics=("arbitrary",), vmem_limit_bytes=VMEM_LIMIT_BYTES),
        name="ab_out_ffn",
    )(*args)


def _sgu_ffn_kernel(final, x_ref, gm_ref, win_ref, lng_ref, lnb_ref, ws_ref, bs_ref,
                    wout_ref, gf_ref, w1_ref, w2_ref, *rest):
    if final:
        gfin_ref, o_ref, gated_ref = rest
    else:
        gfin_ref, (o_ref, gated_ref) = None, rest
    x = x_ref[...]
    h = _rms(x, gm_ref[...]).astype(BF16)
    u_cols = slice(0, SGU_WIDTH)
    v_cols = slice(SGU_WIDTH, 2 * SGU_WIDTH)
    v = jax.nn.gelu(_dot(h, win_ref[:, v_cols]))
    mu = jnp.mean(v, axis=-1, keepdims=True)
    vc = v - mu
    var = jnp.mean(vc * vc, axis=-1, keepdims=True)
    vn = (vc * lax.rsqrt(var + EPS) * lng_ref[...] + lnb_ref[...]).astype(BF16)
    u = jax.nn.gelu(_dot(h, win_ref[:, u_cols]))

    pc = lax.broadcasted_iota(jnp.int32, (SGU_BLOCK, SGU_BLOCK), 0) // CHUNK
    qc = lax.broadcasted_iota(jnp.int32, (SGU_BLOCK, SGU_BLOCK), 1) // CHUNK
    mask = qc <= pc
    for g in range(SGU_GROUPS):
        cols = slice(g * SGU_GROUP_DIM, (g + 1) * SGU_GROUP_DIM)
        w = jnp.where(mask, ws_ref[g], 0.0).astype(BF16)
        bias = bs_ref[:, g:g + 1]
        for n in range(TOKEN_TILE // SGU_BLOCK):
            rows = slice(n * SGU_BLOCK, (n + 1) * SGU_BLOCK)
            mixed = _dot(w, vn[rows, cols]) + bias
            gated_ref[rows, cols] = (u[rows, cols] * mixed).astype(BF16)

    x = x + _dot(gated_ref[...], wout_ref[...])
    _finish(_ffn(x, gf_ref[...], w1_ref, w2_ref), gfin_ref, o_ref)


def _sgu_ffn(x2, g_mix, w_in, ln_g, ln_b, w_s, b_s_t, w_out, g_ffn, w1, w2, g_final):
    t = x2.shape[0]
    row = lambda i: (i, 0)
    final = g_final is not None
    in_specs = [pl.BlockSpec((TOKEN_TILE, D_MODEL), row),
                _resident((1, D_MODEL)), _resident(w_in.shape),
                _resident((1, SGU_WIDTH)), _resident((1, SGU_WIDTH)),
                _resident(w_s.shape), _resident(b_s_t.shape),
                _resident(w_out.shape), _resident((1, D_MODEL)),
                _resident(w1.shape), _resident(w2.shape)]
    args = [x2, g_mix, w_in, ln_g, ln_b, w_s, b_s_t, w_out, g_ffn, w1, w2]
    if final:
        in_specs.append(_resident((1, D_MODEL)))
        args.append(g_final)
    return pl.pallas_call(
        functools.partial(_sgu_ffn_kernel, final),
        grid=(t // TOKEN_TILE,),
        in_specs=in_specs,
        out_specs=pl.BlockSpec((TOKEN_TILE, D_MODEL), row),
        out_shape=jax.ShapeDtypeStruct((t, D_MODEL), F32),
        scratch_shapes=[pltpu.VMEM((TOKEN_TILE, SGU_WIDTH), BF16)],
        compiler_params=pltpu.CompilerParams(
            dimension_semantics=("arbitrary",), vmem_limit_bytes=VMEM_LIMIT_BYTES),
        name="sgu_ffn",
    )(*args)


def kernel(x, norm_mix, norm_ffn, norm_final, ab_w_in, diff_lq1, diff_lk1, diff_lq2,
           diff_lk2, diff_subln, lru_conv_w, lru_conv_b, lru_wa, lru_ba, lru_wx, lru_bx,
           lru_lambda, ab_w_out, c_w_in, c_ln_g, c_ln_b, c_w_s, c_b_s, c_w_out, ffn_w1,
           ffn_w2):
    b, s, d = x.shape
    t = b * s
    assert t % TOKEN_TILE == 0 and s % LRU_TILE == 0 and TOKEN_TILE % SGU_BLOCK == 0
    x2 = x.reshape(t, d)
    vec = lambda a: a.reshape(1, -1)
    for layer in range(DEPTH):
        g_final = vec(norm_final) if layer == DEPTH - 1 else None
        g_ffn = vec(norm_ffn[layer])
        w1 = ffn_w1[layer].astype(BF16)
        w2 = ffn_w2[layer].astype(BF16)
        if layer % 2 == 0:
            e = layer // 2
            lam_init = 0.8 - 0.6 * math.exp(-0.3 * layer)
            q, k, v, xb, gate = _ab_in(x2, vec(norm_mix[layer]), ab_w_in[e].astype(BF16))
            seq = lambda a: a.reshape(b, s, a.shape[-1])
            ya = _diff_attention(seq(q), seq(k), seq(v), vec(diff_lq1[e]), vec(diff_lk1[e]),
                                 vec(diff_lq2[e]), vec(diff_lk2[e]), vec(diff_subln[e]),
                                 lam_init)
            w_gates = jnp.concatenate(
                [_block_diag(lru_wa[e]), _block_diag(lru_wx[e])], axis=1).astype(BF16)
            yb = _rg_lru(seq(xb), seq(gate), lru_conv_w[e], vec(lru_conv_b[e]), w_gates,
                         vec(lru_ba[e]), vec(lru_bx[e]), vec(lru_lambda[e]))
            x2 = _ab_out_ffn(x2, ya.reshape(t, -1), yb.reshape(t, -1),
                             ab_w_out[e].astype(BF16), g_ffn, w1, w2, g_final)
        else:
            o = layer // 2
            x2 = _sgu_ffn(x2, vec(norm_mix[layer]), c_w_in[o].astype(BF16), vec(c_ln_g[o]),
                          vec(c_ln_b[o]), c_w_s[o], c_b_s[o].T, c_w_out[o].astype(BF16),
                          g_ffn, w1, w2, g_final)
    return x2.reshape(b, s, d)
```

```python
import functools
import math

import jax
import jax.numpy as jnp
from jax import lax
from jax.experimental import pallas as pl
from jax.experimental.pallas import tpu as pltpu

D_MODEL = 1024
DEPTH = 4
CHUNK = 64
DIFF_HEADS = 4
DIFF_HEAD_DIM = 64
DIFF_V_DIM = 2 * DIFF_HEAD_DIM
DIFF_WIDTH = DIFF_HEADS * DIFF_V_DIM
LRU_WIDTH = 512
LRU_BLOCKS = 8
LRU_BLOCK_DIM = LRU_WIDTH // LRU_BLOCKS
LRU_C = 8.0
CONV_WIDTH = 4
SGU_WIDTH = D_MODEL
SGU_GROUPS = 8
SGU_GROUP_DIM = SGU_WIDTH // SGU_GROUPS
SGU_BLOCK = 128
D_FF = 4 * D_MODEL
EPS = 1e-6

F32 = jnp.float32
BF16 = jnp.bfloat16

VMEM_LIMIT_BYTES = 56 * 1024 * 1024
SUBLANES = 8

TOKEN_TILE = 512
FF_CHUNK = 1024
Q_TILE = 256
KV_TILE = 256
LRU_TILE = 512
NEG_BIG = -1e30


def _resident(shape):
    return pl.BlockSpec(shape, lambda *_: (0,) * len(shape),
                        pipeline_mode=pl.Buffered(1))


def _rms(x, g):
    return x * lax.rsqrt(jnp.mean(x * x, axis=-1, keepdims=True) + EPS) * g


def _dot(a, b):
    return jnp.dot(a, b, preferred_element_type=F32)


def _dot_nt(a, b):
    return lax.dot_general(a, b, (((1,), (1,)), ((), ())), preferred_element_type=F32)


def _ab_in_kernel(x_ref, g_ref, w_ref, wvt_ref, q_ref, k_ref, vt_ref, xb_ref, gate_ref):
    h = _rms(x_ref[...], g_ref[...]).astype(BF16)
    w = DIFF_WIDTH
    q_ref[...] = (_dot(h, w_ref[:, 0:w]) * (DIFF_HEAD_DIM ** -0.5)).astype(BF16)
    k_ref[...] = _dot(h, w_ref[:, w:2 * w]).astype(BF16)
    xb_ref[...] = _dot(h, w_ref[:, 3 * w:3 * w + LRU_WIDTH])
    gate_ref[...] = _dot(h, w_ref[:, 3 * w + LRU_WIDTH:])
    vt = _dot_nt(wvt_ref[...], h).astype(BF16)
    for c in range(TOKEN_TILE // KV_TILE):
        vt_ref[c] = vt[:, c * KV_TILE:(c + 1) * KV_TILE]


def _ab_in(x2, g, w_in, w_v_t):
    t = x2.shape[0]
    n_in = w_in.shape[1]
    row = lambda i: (i, 0)
    kv_per_tile = TOKEN_TILE // KV_TILE
    return pl.pallas_call(
        _ab_in_kernel,
        grid=(t // TOKEN_TILE,),
        in_specs=[pl.BlockSpec((TOKEN_TILE, D_MODEL), row),
                  _resident((1, D_MODEL)),
                  _resident((D_MODEL, n_in)),
                  _resident((DIFF_WIDTH, D_MODEL))],
        out_specs=[pl.BlockSpec((TOKEN_TILE, DIFF_WIDTH), row)] * 2
        + [pl.BlockSpec((kv_per_tile, DIFF_WIDTH, KV_TILE), lambda i: (i, 0, 0))]
        + [pl.BlockSpec((TOKEN_TILE, LRU_WIDTH), row)] * 2,
        out_shape=[jax.ShapeDtypeStruct((t, DIFF_WIDTH), BF16)] * 2
        + [jax.ShapeDtypeStruct((t // KV_TILE, DIFF_WIDTH, KV_TILE), BF16)]
        + [jax.ShapeDtypeStruct((t, LRU_WIDTH), F32)] * 2,
        compiler_params=pltpu.CompilerParams(
            dimension_semantics=("arbitrary",), vmem_limit_bytes=VMEM_LIMIT_BYTES),
        name="ab_in_proj",
    )(x2, g, w_in, w_v_t)


def _attn_kernel(lam_init, q_ref, k_ref, vt_ref, lq1_ref, lk1_ref, lq2_ref, lk2_ref,
                 sg_ref, o_ref, m_ref, l_ref, acc_ref, s_ref):
    i = pl.program_id(1)
    d = DIFF_HEAD_DIM
    vd = DIFF_V_DIM

    m_ref[...] = jnp.full(m_ref.shape, NEG_BIG, F32)
    l_ref[...] = jnp.zeros(l_ref.shape, F32)
    acc_ref[...] = jnp.zeros(acc_ref.shape, F32)

    lane = lax.broadcasted_iota(jnp.int32, (Q_TILE, vd), 1)
    qz = []
    for h in range(DIFF_HEADS):
        qh = q_ref[:, h * vd:(h + 1) * vd]
        zero = jnp.zeros_like(qh)
        qz.append((jnp.where(lane < d, qh, zero), jnp.where(lane >= d, qh, zero)))

    n_chains = 2 * DIFF_HEADS

    def scores(c, j):
        h, mp = divmod(c, 2)
        start = pl.multiple_of(j * KV_TILE, KV_TILE)
        kb = k_ref[pl.ds(start, KV_TILE), h * vd:(h + 1) * vd]
        return _dot_nt(kb, qz[h][mp])

    def softmax_update(c, s):
        m_prev = m_ref[c]
        m_new = jnp.maximum(m_prev, jnp.max(s, axis=0, keepdims=True))
        alpha = jnp.exp(m_prev - m_new)
        p = jnp.exp(s - m_new)
        l_ref[c] = alpha * l_ref[c] + jnp.sum(p, axis=0, keepdims=True)
        m_ref[c] = m_new
        return p.astype(BF16), alpha

    def accumulate(c, j, p, alpha):
        vb = vt_ref[j, (c // 2) * vd:(c // 2 + 1) * vd, :]
        acc_ref[c] = alpha * acc_ref[c] + _dot(vb, p)

    def step(j, mask, prefetch):
        pending = None
        for c in range(n_chains):
            s = s_ref[c]
            if prefetch:
                s_ref[c] = scores(c, j + 1)
            if mask is not None:
                s = jnp.where(mask, s, NEG_BIG)
            p, alpha = softmax_update(c, s)
            if pending is not None:
                accumulate(*pending)
            pending = (c, j, p, alpha)
        accumulate(*pending)

    for c in range(n_chains):
        s_ref[c] = scores(c, 0)

    def body(j, carry):
        step(j, None, True)
        return carry

    lax.fori_loop(0, i, body, 0)
    kc = lax.broadcasted_iota(jnp.int32, (KV_TILE, Q_TILE), 0) // CHUNK
    qc = lax.broadcasted_iota(jnp.int32, (KV_TILE, Q_TILE), 1) // CHUNK
    step(i, kc <= qc, False)

    lam = (jnp.exp(jnp.sum(lq1_ref[...] * lk1_ref[...], axis=-1, keepdims=True))
           - jnp.exp(jnp.sum(lq2_ref[...] * lk2_ref[...], axis=-1, keepdims=True))
           + lam_init)
    for h in range(DIFF_HEADS):
        o = (acc_ref[2 * h] * (1.0 / l_ref[2 * h])
             - lam * (acc_ref[2 * h + 1] * (1.0 / l_ref[2 * h + 1])))
        o = o * lax.rsqrt(jnp.mean(o * o, axis=0, keepdims=True) + EPS)
        o = o * (sg_ref[...] * (1.0 - lam_init))
        o_ref[:, h * vd:(h + 1) * vd] = o.T.astype(o_ref.dtype)


def _diff_attention(q, k, vt, lq1, lk1, lq2, lk2, subln_col, lam_init):
    b, s, w = q.shape
    assert Q_TILE == KV_TILE and Q_TILE % CHUNK == 0 and s % Q_TILE == 0
    vd = DIFF_V_DIM
    n_maps = 2 * DIFF_HEADS
    vec = lambda n: _resident((1, n))
    return pl.pallas_call(
        functools.partial(_attn_kernel, lam_init),
        grid=(b, s // Q_TILE),
        in_specs=[pl.BlockSpec((None, Q_TILE, w), lambda bi, i: (bi, i, 0)),
                  pl.BlockSpec((None, s, w), lambda bi, i: (bi, 0, 0)),
                  pl.BlockSpec((s // KV_TILE, w, KV_TILE), lambda bi, i: (bi, 0, 0)),
                  vec(DIFF_HEAD_DIM), vec(DIFF_HEAD_DIM), vec(DIFF_HEAD_DIM),
                  vec(DIFF_HEAD_DIM), _resident((vd, 1))],
        out_specs=pl.BlockSpec((None, Q_TILE, w), lambda bi, i: (bi, i, 0)),
        out_shape=jax.ShapeDtypeStruct((b, s, w), BF16),
        scratch_shapes=[pltpu.VMEM((n_maps, 1, Q_TILE), F32),
                        pltpu.VMEM((n_maps, 1, Q_TILE), F32),
                        pltpu.VMEM((n_maps, vd, Q_TILE), F32),
                        pltpu.VMEM((n_maps, KV_TILE, Q_TILE), F32)],
        compiler_params=pltpu.CompilerParams(
            dimension_semantics=("arbitrary",) * 2, vmem_limit_bytes=VMEM_LIMIT_BYTES),
        name="diff_attention",
    )(q, k, vt, lq1, lk1, lq2, lk2, subln_col)


def _lru_kernel(xb_ref, gate_ref, cw_ref, cb_ref, wg_ref, ba_ref, bx_ref, lam_ref,
                y_ref, xpad_ref, h_ref):
    ts = LRU_TILE
    pad = SUBLANES

    @pl.when(pl.program_id(1) == 0)
    def _():
        xpad_ref[0:pad, :] = jnp.zeros((pad, LRU_WIDTH), F32)
        h_ref[...] = jnp.zeros(h_ref.shape, F32)

    xpad_ref[pad:pad + ts, :] = xb_ref[...]
    xc = cb_ref[...]
    for j in range(CONV_WIDTH):
        off = pad - (CONV_WIDTH - 1) + j
        xc = xc + cw_ref[j:j + 1, :] * xpad_ref[off:off + ts, :]
    xpad_ref[0:pad, :] = xpad_ref[ts:ts + pad, :]

    gates = _dot(xc.astype(BF16), wg_ref[...])
    r = jax.nn.sigmoid(gates[:, :LRU_WIDTH] + ba_ref[...])
    ig = jax.nn.sigmoid(gates[:, LRU_WIDTH:] + bx_ref[...])
    z = -lam_ref[...]
    softplus = jnp.maximum(z, 0.0) + jnp.log1p(jnp.exp(-jnp.abs(z)))
    log_a = -LRU_C * r * softplus
    a = jnp.exp(log_a)
    bt = jnp.sqrt(-jnp.tanh(log_a) * (a * a + 1.0)) * (ig * xc)

    row = lax.broadcasted_iota(jnp.int32, (ts, LRU_WIDTH), 0)
    shift = 1
    while shift < ts:
        keep = row >= shift
        a_prev = jnp.where(keep, pltpu.roll(a, shift, 0), 1.0)
        b_prev = jnp.where(keep, pltpu.roll(bt, shift, 0), 0.0)
        bt = a * b_prev + bt
        a = a * a_prev
        shift *= 2
    h = a * h_ref[...] + bt
    h_ref[...] = h[ts - 1:ts, :]
    y_ref[...] = (jax.nn.gelu(gate_ref[...]) * h).astype(y_ref.dtype)


def _rg_lru(xb, gate, conv_w, conv_b, w_gates, ba, bx, lam):
    b, s, c = xb.shape
    seq = lambda bi, si: (bi, si, 0)
    return pl.pallas_call(
        _lru_kernel,
        grid=(b, s // LRU_TILE),
        in_specs=[pl.BlockSpec((None, LRU_TILE, c), seq),
                  pl.BlockSpec((None, LRU_TILE, c), seq),
                  _resident((CONV_WIDTH, c)), _resident((1, c)),
                  _resident((c, 2 * c)), _resident((1, c)), _resident((1, c)),
                  _resident((1, c))],
        out_specs=pl.BlockSpec((None, LRU_TILE, c), seq),
        out_shape=jax.ShapeDtypeStruct((b, s, c), BF16),
        scratch_shapes=[pltpu.VMEM((LRU_TILE + SUBLANES, c), F32),
                        pltpu.VMEM((1, c), F32)],
        compiler_params=pltpu.CompilerParams(
            dimension_semantics=("arbitrary", "arbitrary"),
            vmem_limit_bytes=VMEM_LIMIT_BYTES),
        name="rg_lru",
    )(xb, gate, conv_w, conv_b, w_gates, ba, bx, lam)


def _block_diag(w):
    nb, d, _ = w.shape
    eye = jnp.eye(nb, dtype=w.dtype)
    return (eye[:, None, :, None] * w[:, :, None, :]).reshape(nb * d, nb * d)


def _ffn(x, g, w1_ref, w2_ref):
    h = _rms(x, g).astype(BF16)
    acc = None
    for c in range(D_FF // FF_CHUNK):
        cols = slice(c * FF_CHUNK, (c + 1) * FF_CHUNK)
        t = jnp.maximum(_dot(h, w1_ref[:, cols]), 0.0)
        part = _dot((t * t).astype(BF16), w2_ref[cols, :])
        acc = part if acc is None else acc + part
    return x + acc


def _finish(x, gfin_ref, o_ref):
    if gfin_ref is not None:
        x = _rms(x, gfin_ref[...])
    o_ref[...] = x


def _ab_out_ffn_kernel(final, x_ref, ya_ref, yb_ref, wo_ref, gf_ref, w1_ref, w2_ref,
                       *rest):
    gfin_ref, o_ref = rest if final else (None, rest[0])
    y = jnp.concatenate([ya_ref[...], yb_ref[...]], axis=1)
    x = x_ref[...] + _dot(y, wo_ref[...])
    _finish(_ffn(x, gf_ref[...], w1_ref, w2_ref), gfin_ref, o_ref)


def _ab_out_ffn(x2, ya, yb, w_out, g_ffn, w1, w2, g_final):
    t = x2.shape[0]
    row = lambda i: (i, 0)
    final = g_final is not None
    in_specs = [pl.BlockSpec((TOKEN_TILE, D_MODEL), row),
                pl.BlockSpec((TOKEN_TILE, DIFF_WIDTH), row),
                pl.BlockSpec((TOKEN_TILE, LRU_WIDTH), row),
                _resident(w_out.shape), _resident((1, D_MODEL)),
                _resident(w1.shape), _resident(w2.shape)]
    args = [x2, ya, yb, w_out, g_ffn, w1, w2]
    if final:
        in_specs.append(_resident((1, D_MODEL)))
        args.append(g_final)
    return pl.pallas_call(
        functools.partial(_ab_out_ffn_kernel, final),
        grid=(t // TOKEN_TILE,),
        in_specs=in_specs,
        out_specs=pl.BlockSpec((TOKEN_TILE, D_MODEL), row),
        out_shape=jax.ShapeDtypeStruct((t, D_MODEL), F32),
        compiler_params=pltpu.CompilerParams(
            dimension_semantics=("arbitrary",), vmem_limit_bytes=VMEM_LIMIT_BYTES),
        name="ab_out_ffn",
    )(*args)


def _sgu_ffn_kernel(final, x_ref, gm_ref, win_ref, lng_ref, lnb_ref, ws_ref, bs_ref,
                    wout_ref, gf_ref, w1_ref, w2_ref, *rest):
    if final:
        gfin_ref, o_ref, gated_ref = rest
    else:
        gfin_ref, (o_ref, gated_ref) = None, rest
    x = x_ref[...]
    h = _rms(x, gm_ref[...]).astype(BF16)
    u_cols = slice(0, SGU_WIDTH)
    v_cols = slice(SGU_WIDTH, 2 * SGU_WIDTH)
    v = jax.nn.gelu(_dot(h, win_ref[:, v_cols]))
    mu = jnp.mean(v, axis=-1, keepdims=True)
    vc = v - mu
    var = jnp.mean(vc * vc, axis=-1, keepdims=True)
    vn = (vc * lax.rsqrt(var + EPS) * lng_ref[...] + lnb_ref[...]).astype(BF16)
    u = jax.nn.gelu(_dot(h, win_ref[:, u_cols]))

    pc = lax.broadcasted_iota(jnp.int32, (SGU_BLOCK, SGU_BLOCK), 0) // CHUNK
    qc = lax.broadcasted_iota(jnp.int32, (SGU_BLOCK, SGU_BLOCK), 1) // CHUNK
    mask = qc <= pc
    for g in range(SGU_GROUPS):
        cols = slice(g * SGU_GROUP_DIM, (g + 1) * SGU_GROUP_DIM)
        w = jnp.where(mask, ws_ref[g], 0.0).astype(BF16)
        bias = bs_ref[:, g:g + 1]
        for n in range(TOKEN_TILE // SGU_BLOCK):
            rows = slice(n * SGU_BLOCK, (n + 1) * SGU_BLOCK)
            mixed = _dot(w, vn[rows, cols]) + bias
            gated_ref[rows, cols] = (u[rows, cols] * mixed).astype(BF16)

    x = x + _dot(gated_ref[...], wout_ref[...])
    _finish(_ffn(x, gf_ref[...], w1_ref, w2_ref), gfin_ref, o_ref)


def _sgu_ffn(x2, g_mix, w_in, ln_g, ln_b, w_s, b_s_t, w_out, g_ffn, w1, w2, g_final):
    t = x2.shape[0]
    row = lambda i: (i, 0)
    final = g_final is not None
    in_specs = [pl.BlockSpec((TOKEN_TILE, D_MODEL), row),
                _resident((1, D_MODEL)), _resident(w_in.shape),
                _resident((1, SGU_WIDTH)), _resident((1, SGU_WIDTH)),
                _resident(w_s.shape), _resident(b_s_t.shape),
                _resident(w_out.shape), _resident((1, D_MODEL)),
                _resident(w1.shape), _resident(w2.shape)]
    args = [x2, g_mix, w_in, ln_g, ln_b, w_s, b_s_t, w_out, g_ffn, w1, w2]
    if final:
        in_specs.append(_resident((1, D_MODEL)))
        args.append(g_final)
    return pl.pallas_call(
        functools.partial(_sgu_ffn_kernel, final),
        grid=(t // TOKEN_TILE,),
        in_specs=in_specs,
        out_specs=pl.BlockSpec((TOKEN_TILE, D_MODEL), row),
        out_shape=jax.ShapeDtypeStruct((t, D_MODEL), F32),
        scratch_shapes=[pltpu.VMEM((TOKEN_TILE, SGU_WIDTH), BF16)],
        compiler_params=pltpu.CompilerParams(
            dimension_semantics=("arbitrary",), vmem_limit_bytes=VMEM_LIMIT_BYTES),
        name="sgu_ffn",
    )(*args)


def kernel(x, norm_mix, norm_ffn, norm_final, ab_w_in, diff_lq1, diff_lk1, diff_lq2,
           diff_lk2, diff_subln, lru_conv_w, lru_conv_b, lru_wa, lru_ba, lru_wx, lru_bx,
           lru_lambda, ab_w_out, c_w_in, c_ln_g, c_ln_b, c_w_s, c_b_s, c_w_out, ffn_w1,
           ffn_w2):
    b, s, d = x.shape
    t = b * s
    assert t % TOKEN_TILE == 0 and s % LRU_TILE == 0 and TOKEN_TILE % SGU_BLOCK == 0
    x2 = x.reshape(t, d)
    vec = lambda a: a.reshape(1, -1)
    for layer in range(DEPTH):
        g_final = vec(norm_final) if layer == DEPTH - 1 else None
        g_ffn = vec(norm_ffn[layer])
        w1 = ffn_w1[layer].astype(BF16)
        w2 = ffn_w2[layer].astype(BF16)
        if layer % 2 == 0:
            e = layer // 2
            lam_init = 0.8 - 0.6 * math.exp(-0.3 * layer)
            w_in = ab_w_in[e].astype(BF16)
            w_v_t = w_in[:, 2 * DIFF_WIDTH:3 * DIFF_WIDTH].T
            q, k, vt, xb, gate = _ab_in(x2, vec(norm_mix[layer]), w_in, w_v_t)
            seq = lambda a: a.reshape(b, s, a.shape[-1])
            ya = _diff_attention(seq(q), seq(k), vt, vec(diff_lq1[e]), vec(diff_lk1[e]),
                                 vec(diff_lq2[e]), vec(diff_lk2[e]),
                                 diff_subln[e].reshape(-1, 1), lam_init)
            w_gates = jnp.concatenate(
                [_block_diag(lru_wa[e]), _block_diag(lru_wx[e])], axis=1).astype(BF16)
            yb = _rg_lru(seq(xb), seq(gate), lru_conv_w[e], vec(lru_conv_b[e]), w_gates,
                         vec(lru_ba[e]), vec(lru_bx[e]), vec(lru_lambda[e]))
            x2 = _ab_out_ffn(x2, ya.reshape(t, -1), yb.reshape(t, -1),
                             ab_w_out[e].astype(BF16), g_ffn, w1, w2, g_final)
        else:
            o = layer // 2
            x2 = _sgu_ffn(x2, vec(norm_mix[layer]), c_w_in[o].astype(BF16), vec(c_ln_g[o]),
                          vec(c_ln_b[o]), c_w_s[o], c_b_s[o].T, c_w_out[o].astype(BF16),
                          g_ffn, w1, w2, g_final)
    return x2.reshape(b, s, d)
```

```python
import functools
import math

import jax
import jax.numpy as jnp
from jax import lax
from jax.experimental import pallas as pl
from jax.experimental.pallas import tpu as pltpu

D_MODEL = 1024
DEPTH = 4
CHUNK = 64
DIFF_HEADS = 4
DIFF_HEAD_DIM = 64
DIFF_V_DIM = 2 * DIFF_HEAD_DIM
DIFF_WIDTH = DIFF_HEADS * DIFF_V_DIM
LRU_WIDTH = 512
LRU_BLOCKS = 8
LRU_BLOCK_DIM = LRU_WIDTH // LRU_BLOCKS
LRU_C = 8.0
CONV_WIDTH = 4
SGU_WIDTH = D_MODEL
SGU_GROUPS = 8
SGU_GROUP_DIM = SGU_WIDTH // SGU_GROUPS
SGU_BLOCK = 128
D_FF = 4 * D_MODEL
EPS = 1e-6

F32 = jnp.float32
BF16 = jnp.bfloat16

VMEM_LIMIT_BYTES = 56 * 1024 * 1024
SUBLANES = 8
BF16_SUBLANES = 16
LOG2_E = math.log2(math.e)

TOKEN_TILE = 512
FF_CHUNK = 1024
Q_TILE = 256
KV_TILE = 256
LRU_ROWS = 128
NEG_BIG = -1e30


def _resident(shape):
    return pl.BlockSpec(shape, lambda *_: (0,) * len(shape),
                        pipeline_mode=pl.Buffered(1))


def _rms(x, g):
    return x * lax.rsqrt(jnp.mean(x * x, axis=-1, keepdims=True) + EPS) * g


def _sigmoid(x):
    return 0.5 * jnp.tanh(0.5 * x) + 0.5


def _dot(a, b):
    return jnp.dot(a, b, preferred_element_type=F32)


def _dot_nt(a, b):
    return lax.dot_general(a, b, (((1,), (1,)), ((), ())), preferred_element_type=F32)


def _ab_in_kernel(tiles_per_seq, x_ref, g_ref, w_ref, wvt_ref, cw_ref, cb_ref, wg_ref,
                  ba_ref, bx_ref, lam_ref, q_ref, k_ref, vt_ref, yb_ref, *lru_scratch):
    i = pl.program_id(0)
    xpad_ref, gate_ref, hstate_ref, aloc_ref, bloc_ref = lru_scratch

    @pl.when(i == 0)
    def _():
        xpad_ref[...] = jnp.zeros(xpad_ref.shape, F32)
        gate_ref[...] = jnp.zeros(gate_ref.shape, F32)
        hstate_ref[...] = jnp.zeros(hstate_ref.shape, F32)

    @pl.when(lax.rem(i - 1, tiles_per_seq) == 0)
    def _():
        xpad_ref[0:SUBLANES, :] = jnp.zeros((SUBLANES, LRU_WIDTH), F32)
        hstate_ref[...] = jnp.zeros(hstate_ref.shape, F32)

    h = _rms(x_ref[...], g_ref[...]).astype(BF16)
    w = DIFF_WIDTH
    q_scale = DIFF_HEAD_DIM ** -0.5 * LOG2_E
    lru_refs = (cw_ref, cb_ref, wg_ref, ba_ref, bx_ref, lam_ref, xpad_ref, aloc_ref, bloc_ref)
    assert TOKEN_TILE // LRU_ROWS == 4
    ser_refs = (gate_ref, aloc_ref, bloc_ref, yb_ref)
    assert TOKEN_TILE // KV_TILE == 2

    def v_transposed(c):
        toks = slice(c * KV_TILE, (c + 1) * KV_TILE)
        vt_ref[c] = _dot_nt(wvt_ref[...], h[toks]).astype(BF16)

    state = hstate_ref[...]
    _lru_rows(0, *lru_refs)
    q_ref[...] = (_dot(h, w_ref[:, 0:w]) * q_scale).astype(BF16)
    v_transposed(0)
    state = _lru_serial(0, state, *ser_refs)
    _lru_rows(1, *lru_refs)
    k_ref[...] = _dot(h, w_ref[:, w:2 * w]).astype(BF16)
    v_transposed(1)
    state = _lru_serial(1, state, *ser_refs)
    _lru_rows(2, *lru_refs)
    xb_next = _dot(h, w_ref[:, 3 * w:3 * w + LRU_WIDTH])
    state = _lru_serial(2, state, *ser_refs)
    _lru_rows(3, *lru_refs)
    gate_next = _dot(h, w_ref[:, 3 * w + LRU_WIDTH:])
    hstate_ref[...] = _lru_serial(3, state, *ser_refs)
    xpad_ref[0:SUBLANES, :] = xpad_ref[TOKEN_TILE:TOKEN_TILE + SUBLANES, :]
    xpad_ref[SUBLANES:SUBLANES + TOKEN_TILE, :] = xb_next
    gate_ref[...] = gate_next


def _ab_in(x2, seq_len, g, w_in, w_v_t, conv_w, conv_b, w_gates, ba, bx, lam):
    t = x2.shape[0]
    n_in = w_in.shape[1]
    c = LRU_WIDTH
    assert seq_len % TOKEN_TILE == 0
    n_tiles = t // TOKEN_TILE
    row = lambda i: (jnp.minimum(i, n_tiles - 1), 0)
    prev_row = lambda i: (jnp.maximum(i - 1, 0), 0)
    kv_per_tile = TOKEN_TILE // KV_TILE
    return pl.pallas_call(
        functools.partial(_ab_in_kernel, seq_len // TOKEN_TILE),
        grid=(n_tiles + 1,),
        in_specs=[pl.BlockSpec((TOKEN_TILE, D_MODEL), row),
                  _resident((1, D_MODEL)),
                  _resident((D_MODEL, n_in)),
                  _resident((DIFF_WIDTH, D_MODEL)),
                  _resident((CONV_WIDTH, c)), _resident((1, c)),
                  _resident((c, 2 * c)), _resident((1, c)), _resident((1, c)),
                  _resident((1, c))],
        out_specs=[pl.BlockSpec((TOKEN_TILE, DIFF_WIDTH), row)] * 2
        + [pl.BlockSpec((kv_per_tile, DIFF_WIDTH, KV_TILE), lambda i: row(i) + (0,))]
        + [pl.BlockSpec((TOKEN_TILE, c), prev_row)],
        out_shape=[jax.ShapeDtypeStruct((t, DIFF_WIDTH), BF16)] * 2
        + [jax.ShapeDtypeStruct((t // KV_TILE, DIFF_WIDTH, KV_TILE), BF16)]
        + [jax.ShapeDtypeStruct((t, c), BF16)],
        scratch_shapes=[pltpu.VMEM((TOKEN_TILE + SUBLANES, c), F32),
                        pltpu.VMEM((TOKEN_TILE, c), F32),
                        pltpu.VMEM((SUBLANES, c), F32)]
        + [pltpu.VMEM((TOKEN_TILE // SUBLANES, SUBLANES, c), F32)] * 2,
        compiler_params=pltpu.CompilerParams(
            dimension_semantics=("arbitrary",), vmem_limit_bytes=VMEM_LIMIT_BYTES),
        name="ab_in_lru",
    )(x2, g, w_in, w_v_t, conv_w, conv_b, w_gates, ba, bx, lam)


def _attn_kernel(lam_init, q_ref, k_ref, vt_ref, lq1_ref, lk1_ref, lq2_ref, lk2_ref,
                 sg_ref, o_ref, m_ref, acc_ref, s_ref):
    i = pl.program_id(1)
    d = DIFF_HEAD_DIM
    vd = DIFF_V_DIM

    m_ref[...] = jnp.full(m_ref.shape, NEG_BIG, F32)
    acc_ref[...] = jnp.zeros(acc_ref.shape, F32)
    ones = jnp.ones((BF16_SUBLANES, KV_TILE), BF16)

    lane = lax.broadcasted_iota(jnp.int32, (Q_TILE, vd), 1)
    qz = []
    for h in range(DIFF_HEADS):
        qh = q_ref[:, h * vd:(h + 1) * vd]
        zero = jnp.zeros_like(qh)
        qz.append((jnp.where(lane < d, qh, zero), jnp.where(lane >= d, qh, zero)))

    n_chains = 2 * DIFF_HEADS

    def scores(c, j):
        h, mp = divmod(c, 2)
        start = pl.multiple_of(j * KV_TILE, KV_TILE)
        kb = k_ref[pl.ds(start, KV_TILE), h * vd:(h + 1) * vd]
        return _dot_nt(kb, qz[h][mp])

    def softmax_update(c, s):
        m_prev = m_ref[c]
        m_new = jnp.maximum(m_prev, jnp.max(s, axis=0, keepdims=True))
        alpha = jnp.exp2(m_prev - m_new)
        p = jnp.exp2(s - m_new)
        m_ref[c] = m_new
        return p.astype(BF16), alpha

    def accumulate(c, j, p, alpha):
        vb = vt_ref[j, (c // 2) * vd:(c // 2 + 1) * vd, :]
        acc_ref[c] = alpha * acc_ref[c] + _dot(jnp.concatenate([vb, ones], axis=0), p)

    def step(j, mask, prefetch):
        pending = None
        for c in range(n_chains):
            s = s_ref[c]
            if prefetch:
                s_ref[c] = scores(c, j + 1)
            if mask is not None:
                s = jnp.where(mask, s, NEG_BIG)
            p, alpha = softmax_update(c, s)
            if pending is not None:
                accumulate(*pending)
            pending = (c, j, p, alpha)
        accumulate(*pending)

    for c in range(n_chains):
        s_ref[c] = scores(c, 0)

    def body(j, carry):
        step(j, None, True)
        return carry

    lax.fori_loop(0, i, body, 0)
    kc = lax.broadcasted_iota(jnp.int32, (KV_TILE, Q_TILE), 0) // CHUNK
    qc = lax.broadcasted_iota(jnp.int32, (KV_TILE, Q_TILE), 1) // CHUNK
    step(i, kc <= qc, False)

    lam = (jnp.exp(jnp.sum(lq1_ref[...] * lk1_ref[...], axis=-1, keepdims=True))
           - jnp.exp(jnp.sum(lq2_ref[...] * lk2_ref[...], axis=-1, keepdims=True))
           + lam_init)
    def normalized(c):
        return acc_ref[c, 0:vd, :] * (1.0 / acc_ref[c, vd:vd + 1, :])

    for h in range(DIFF_HEADS):
        o = normalized(2 * h) - lam * normalized(2 * h + 1)
        o = o * lax.rsqrt(jnp.mean(o * o, axis=0, keepdims=True) + EPS)
        o = o * (sg_ref[...] * (1.0 - lam_init))
        o_ref[:, h * vd:(h + 1) * vd] = o.T.astype(o_ref.dtype)


def _diff_attention(q, k, vt, lq1, lk1, lq2, lk2, subln_col, lam_init):
    b, s, w = q.shape
    assert Q_TILE == KV_TILE and Q_TILE % CHUNK == 0 and s % Q_TILE == 0
    vd = DIFF_V_DIM
    n_maps = 2 * DIFF_HEADS
    vec = lambda n: _resident((1, n))
    return pl.pallas_call(
        functools.partial(_attn_kernel, lam_init),
        grid=(b, s // Q_TILE),
        in_specs=[pl.BlockSpec((None, Q_TILE, w), lambda bi, i: (bi, i, 0)),
                  pl.BlockSpec((None, s, w), lambda bi, i: (bi, 0, 0)),
                  pl.BlockSpec((s // KV_TILE, w, KV_TILE), lambda bi, i: (bi, 0, 0)),
                  vec(DIFF_HEAD_DIM), vec(DIFF_HEAD_DIM), vec(DIFF_HEAD_DIM),
                  vec(DIFF_HEAD_DIM), _resident((vd, 1))],
        out_specs=pl.BlockSpec((None, Q_TILE, w), lambda bi, i: (bi, i, 0)),
        out_shape=jax.ShapeDtypeStruct((b, s, w), BF16),
        scratch_shapes=[pltpu.VMEM((n_maps, 1, Q_TILE), F32),
                        pltpu.VMEM((n_maps, vd + BF16_SUBLANES, Q_TILE), F32),
                        pltpu.VMEM((n_maps, KV_TILE, Q_TILE), F32)],
        compiler_params=pltpu.CompilerParams(
            dimension_semantics=("arbitrary",) * 2, vmem_limit_bytes=VMEM_LIMIT_BYTES),
        name="diff_attention",
    )(q, k, vt, lq1, lk1, lq2, lk2, subln_col)


def _lru_rows(slab, cw_ref, cb_ref, wg_ref, ba_ref, bx_ref, lam_ref, xpad_ref, aloc_ref,
              bloc_ref):
    pad = SUBLANES
    base = slab * LRU_ROWS
    xc = cb_ref[...]
    for j in range(CONV_WIDTH):
        off = base + pad - (CONV_WIDTH - 1) + j
        xc = xc + cw_ref[j:j + 1, :] * xpad_ref[off:off + LRU_ROWS, :]

    gates = _dot(xc.astype(BF16), wg_ref[...])
    r = _sigmoid(gates[:, :LRU_WIDTH] + ba_ref[...])
    ig = _sigmoid(gates[:, LRU_WIDTH:] + bx_ref[...])
    z = -lam_ref[...]
    softplus = jnp.maximum(z, 0.0) + jnp.log1p(jnp.exp(-jnp.abs(z)))
    log_a = -LRU_C * r * softplus
    a = jnp.exp(log_a)
    bt = jnp.sqrt(-jnp.tanh(log_a) * (a * a + 1.0)) * (ig * xc)

    groups = LRU_ROWS // SUBLANES
    a3 = a.reshape(groups, SUBLANES, LRU_WIDTH)
    b3 = bt.reshape(groups, SUBLANES, LRU_WIDTH)
    sub = lax.broadcasted_iota(jnp.int32, a3.shape, 1)
    shift = 1
    while shift < SUBLANES:
        keep = sub >= shift
        a_prev = jnp.where(keep, pltpu.roll(a3, shift, 1), 1.0)
        b_prev = jnp.where(keep, pltpu.roll(b3, shift, 1), 0.0)
        b3 = a3 * b_prev + b3
        a3 = a3 * a_prev
        shift *= 2
    aloc_ref[slab * groups:(slab + 1) * groups] = a3
    bloc_ref[slab * groups:(slab + 1) * groups] = b3


def _lru_serial(slab, h_in, gate_ref, aloc_ref, bloc_ref, yb_ref):
    groups = LRU_ROWS // SUBLANES
    hs = []
    for g in range(slab * groups, (slab + 1) * groups):
        hg = aloc_ref[g] * h_in + bloc_ref[g]
        hs.append(hg)
        h_in = jnp.broadcast_to(hg[SUBLANES - 1:SUBLANES, :], hg.shape)
    rows = slice(slab * LRU_ROWS, (slab + 1) * LRU_ROWS)
    yb = jax.nn.gelu(gate_ref[rows, :]) * jnp.concatenate(hs, axis=0)
    yb_ref[rows, :] = yb.astype(yb_ref.dtype)
    return h_in


def _block_diag(w):
    nb, d, _ = w.shape
    eye = jnp.eye(nb, dtype=w.dtype)
    return (eye[:, None, :, None] * w[:, :, None, :]).reshape(nb * d, nb * d)


def _ffn(x, g, w1_ref, w2_ref):
    h = _rms(x, g).astype(BF16)
    acc = None
    for c in range(D_FF // FF_CHUNK):
        cols = slice(c * FF_CHUNK, (c + 1) * FF_CHUNK)
        t = jnp.maximum(_dot(h, w1_ref[:, cols]), 0.0)
        part = _dot((t * t).astype(BF16), w2_ref[cols, :])
        acc = part if acc is None else acc + part
    return x + acc


def _finish(x, gfin_ref, o_ref):
    if gfin_ref is not None:
        x = _rms(x, gfin_ref[...])
    o_ref[...] = x


def _ab_out_ffn_kernel(final, x_ref, ya_ref, yb_ref, wo_ref, gf_ref, w1_ref, w2_ref,
                       *rest):
    gfin_ref, o_ref = rest if final else (None, rest[0])
    y = jnp.concatenate([ya_ref[...], yb_ref[...]], axis=1)
    x = x_ref[...] + _dot(y, wo_ref[...])
    _finish(_ffn(x, gf_ref[...], w1_ref, w2_ref), gfin_ref, o_ref)


def _ab_out_ffn(x2, ya, yb, w_out, g_ffn, w1, w2, g_final):
    t = x2.shape[0]
    row = lambda i: (i, 0)
    final = g_final is not None
    in_specs = [pl.BlockSpec((TOKEN_TILE, D_MODEL), row),
                pl.BlockSpec((TOKEN_TILE, DIFF_WIDTH), row),
                pl.BlockSpec((TOKEN_TILE, LRU_WIDTH), row),
                _resident(w_out.shape), _resident((1, D_MODEL)),
                _resident(w1.shape), _resident(w2.shape)]
    args = [x2, ya, yb, w_out, g_ffn, w1, w2]
    if final:
        in_specs.append(_resident((1, D_MODEL)))
        args.append(g_final)
    return pl.pallas_call(
        functools.partial(_ab_out_ffn_kernel, final),
        grid=(t // TOKEN_TILE,),
        in_specs=in_specs,
        out_specs=pl.BlockSpec((TOKEN_TILE, D_MODEL), row),
        out_shape=jax.ShapeDtypeStruct((t, D_MODEL), F32),
        compiler_params=pltpu.CompilerParams(
            dimension_semantics=("arbitrary",), vmem_limit_bytes=VMEM_LIMIT_BYTES),
        name="ab_out_ffn",
    )(*args)


def _sgu_ffn_kernel(final, x_ref, gm_ref, win_ref, lng_ref, lnb_ref, ws_ref, bs_ref,
                    wout_ref, gf_ref, w1_ref, w2_ref, *rest):
    if final:
        gfin_ref, o_ref, gated_ref = rest
    else:
        gfin_ref, (o_ref, gated_ref) = None, rest
    x = x_ref[...]
    h = _rms(x, gm_ref[...]).astype(BF16)
    u_cols = slice(0, SGU_WIDTH)
    v_cols = slice(SGU_WIDTH, 2 * SGU_WIDTH)
    v = jax.nn.gelu(_dot(h, win_ref[:, v_cols]))
    mu = jnp.mean(v, axis=-1, keepdims=True)
    vc = v - mu
    var = jnp.mean(vc * vc, axis=-1, keepdims=True)
    vn = (vc * lax.rsqrt(var + EPS) * lng_ref[...] + lnb_ref[...]).astype(BF16)
    u = jax.nn.gelu(_dot(h, win_ref[:, u_cols]))

    pc = lax.broadcasted_iota(jnp.int32, (SGU_BLOCK, SGU_BLOCK), 0) // CHUNK
    qc = lax.broadcasted_iota(jnp.int32, (SGU_BLOCK, SGU_BLOCK), 1) // CHUNK
    mask = qc <= pc
    for g in range(SGU_GROUPS):
        cols = slice(g * SGU_GROUP_DIM, (g + 1) * SGU_GROUP_DIM)
        w = jnp.where(mask, ws_ref[g], 0.0).astype(BF16)
        bias = bs_ref[:, g:g + 1]
        for n in range(TOKEN_TILE // SGU_BLOCK):
            rows = slice(n * SGU_BLOCK, (n + 1) * SGU_BLOCK)
            mixed = _dot(w, vn[rows, cols]) + bias
            gated_ref[rows, cols] = (u[rows, cols] * mixed).astype(BF16)

    x = x + _dot(gated_ref[...], wout_ref[...])
    _finish(_ffn(x, gf_ref[...], w1_ref, w2_ref), gfin_ref, o_ref)


def _sgu_ffn(x2, g_mix, w_in, ln_g, ln_b, w_s, b_s_t, w_out, g_ffn, w1, w2, g_final):
    t = x2.shape[0]
    row = lambda i: (i, 0)
    final = g_final is not None
    in_specs = [pl.BlockSpec((TOKEN_TILE, D_MODEL), row),
                _resident((1, D_MODEL)), _resident(w_in.shape),
                _resident((1, SGU_WIDTH)), _resident((1, SGU_WIDTH)),
                _resident(w_s.shape), _resident(b_s_t.shape),
                _resident(w_out.shape), _resident((1, D_MODEL)),
                _resident(w1.shape), _resident(w2.shape)]
    args = [x2, g_mix, w_in, ln_g, ln_b, w_s, b_s_t, w_out, g_ffn, w1, w2]
    if final:
        in_specs.append(_resident((1, D_MODEL)))
        args.append(g_final)
    return pl.pallas_call(
        functools.partial(_sgu_ffn_kernel, final),
        grid=(t // TOKEN_TILE,),
        in_specs=in_specs,
        out_specs=pl.BlockSpec((TOKEN_TILE, D_MODEL), row),
        out_shape=jax.ShapeDtypeStruct((t, D_MODEL), F32),
        scratch_shapes=[pltpu.VMEM((TOKEN_TILE, SGU_WIDTH), BF16)],
        compiler_params=pltpu.CompilerParams(
            dimension_semantics=("arbitrary",), vmem_limit_bytes=VMEM_LIMIT_BYTES),
        name="sgu_ffn",
    )(*args)


def kernel(x, norm_mix, norm_ffn, norm_final, ab_w_in, diff_lq1, diff_lk1, diff_lq2,
           diff_lk2, diff_subln, lru_conv_w, lru_conv_b, lru_wa, lru_ba, lru_wx, lru_bx,
           lru_lambda, ab_w_out, c_w_in, c_ln_g, c_ln_b, c_w_s, c_b_s, c_w_out, ffn_w1,
           ffn_w2):
    b, s, d = x.shape
    t = b * s
    assert t % TOKEN_TILE == 0 and TOKEN_TILE % SGU_BLOCK == 0
    x2 = x.reshape(t, d)
    vec = lambda a: a.reshape(1, -1)
    for layer in range(DEPTH):
        g_final = vec(norm_final) if layer == DEPTH - 1 else None
        g_ffn = vec(norm_ffn[layer])
        w1 = ffn_w1[layer].astype(BF16)
        w2 = ffn_w2[layer].astype(BF16)
        if layer % 2 == 0:
            e = layer // 2
            lam_init = 0.8 - 0.6 * math.exp(-0.3 * layer)
            w_in = ab_w_in[e].astype(BF16)
            w_v_t = w_in[:, 2 * DIFF_WIDTH:3 * DIFF_WIDTH].T
            w_gates = jnp.concatenate(
                [_block_diag(lru_wa[e]), _block_diag(lru_wx[e])], axis=1).astype(BF16)
            q, k, vt, yb = _ab_in(x2, s, vec(norm_mix[layer]), w_in, w_v_t, lru_conv_w[e],
                                  vec(lru_conv_b[e]), w_gates, vec(lru_ba[e]),
                                  vec(lru_bx[e]), vec(lru_lambda[e]))
            seq = lambda a: a.reshape(b, s, a.shape[-1])
            ya = _diff_attention(seq(q), seq(k), vt, vec(diff_lq1[e]), vec(diff_lk1[e]),
                                 vec(diff_lq2[e]), vec(diff_lk2[e]),
                                 diff_subln[e].reshape(-1, 1), lam_init)
            x2 = _ab_out_ffn(x2, ya.reshape(t, -1), yb, ab_w_out[e].astype(BF16), g_ffn,
                             w1, w2, g_final)
        else:
            o = layer // 2
            x2 = _sgu_ffn(x2, vec(norm_mix[layer]), c_w_in[o].astype(BF16), vec(c_ln_g[o]),
                          vec(c_ln_b[o]), c_w_s[o], c_b_s[o].T, c_w_out[o].astype(BF16),
                          g_ffn, w1, w2, g_final)
    return x2.reshape(b, s, d)
```

```python
import functools
import math

import jax
import jax.numpy as jnp
from jax import lax
from jax.experimental import pallas as pl
from jax.experimental.pallas import tpu as pltpu

D_MODEL = 1024
DEPTH = 4
CHUNK = 64
DIFF_HEADS = 4
DIFF_HEAD_DIM = 64
DIFF_V_DIM = 2 * DIFF_HEAD_DIM
DIFF_WIDTH = DIFF_HEADS * DIFF_V_DIM
LRU_WIDTH = 512
LRU_BLOCKS = 8
LRU_BLOCK_DIM = LRU_WIDTH // LRU_BLOCKS
LRU_C = 8.0
CONV_WIDTH = 4
SGU_WIDTH = D_MODEL
SGU_GROUPS = 8
SGU_GROUP_DIM = SGU_WIDTH // SGU_GROUPS
SGU_BLOCK = 128
D_FF = 4 * D_MODEL
EPS = 1e-6

F32 = jnp.float32
BF16 = jnp.bfloat16

VMEM_LIMIT_BYTES = 56 * 1024 * 1024
SUBLANES = 8
BF16_SUBLANES = 16
LOG2_E = math.log2(math.e)

TOKEN_TILE = 512
MLP_TILE = 1024
FF_CHUNK = 1024
Q_TILE = 256
Q_GROUPS = 2
KV_TILE = 256
LRU_ROWS = 128
NEG_BIG = -1e30


def _resident(shape):
    return pl.BlockSpec(shape, lambda *_: (0,) * len(shape),
                        pipeline_mode=pl.Buffered(1))


def _rms(x, g):
    return x * lax.rsqrt(jnp.mean(x * x, axis=-1, keepdims=True) + EPS) * g


def _sigmoid(x):
    return 0.5 * jnp.tanh(0.5 * x) + 0.5


def _dot(a, b):
    return jnp.dot(a, b, preferred_element_type=F32)


def _dot_nt(a, b):
    return lax.dot_general(a, b, (((1,), (1,)), ((), ())), preferred_element_type=F32)


def _ab_in_kernel(tiles_per_seq, x_ref, g_ref, w_ref, wvt_ref, cw_ref, cb_ref, wg_ref,
                  ba_ref, bx_ref, lam_ref, q_ref, k_ref, vt_ref, yb_ref, *lru_scratch):
    i = pl.program_id(0)
    xpad_ref, gate_ref, hstate_ref, aloc_ref, bloc_ref = lru_scratch

    @pl.when(i == 0)
    def _():
        xpad_ref[...] = jnp.zeros(xpad_ref.shape, F32)
        gate_ref[...] = jnp.zeros(gate_ref.shape, F32)
        hstate_ref[...] = jnp.zeros(hstate_ref.shape, F32)

    @pl.when(lax.rem(i - 1, tiles_per_seq) == 0)
    def _():
        xpad_ref[0:SUBLANES, :] = jnp.zeros((SUBLANES, LRU_WIDTH), F32)
        hstate_ref[...] = jnp.zeros(hstate_ref.shape, F32)

    h = _rms(x_ref[...], g_ref[...]).astype(BF16)
    w = DIFF_WIDTH
    q_scale = DIFF_HEAD_DIM ** -0.5 * LOG2_E
    conv_refs = (cw_ref, cb_ref, wg_ref, xpad_ref)
    row_refs = (ba_ref, bx_ref, lam_ref, aloc_ref, bloc_ref)
    assert TOKEN_TILE // LRU_ROWS == 4

    def lru_rows(slab):
        xc, gates = _lru_conv_gates(slab, *conv_refs)
        _lru_rows(slab, xc, gates, *row_refs)

    ser_refs = (gate_ref, aloc_ref, bloc_ref, yb_ref)
    assert TOKEN_TILE // KV_TILE == 2

    def v_transposed(c):
        toks = slice(c * KV_TILE, (c + 1) * KV_TILE)
        vt_ref[c] = _dot_nt(wvt_ref[...], h[toks]).astype(BF16)

    state = hstate_ref[...]
    lru_rows(0)
    q_ref[...] = (_dot(h, w_ref[:, 0:w]) * q_scale).astype(BF16)
    v_transposed(0)
    state = _lru_serial(0, state, *ser_refs)
    lru_rows(1)
    k_ref[...] = _dot(h, w_ref[:, w:2 * w]).astype(BF16)
    v_transposed(1)
    state = _lru_serial(1, state, *ser_refs)
    lru_rows(2)
    xb_next = _dot(h, w_ref[:, 3 * w:3 * w + LRU_WIDTH])
    state = _lru_serial(2, state, *ser_refs)
    lru_rows(3)
    gate_next = _dot(h, w_ref[:, 3 * w + LRU_WIDTH:])
    hstate_ref[...] = _lru_serial(3, state, *ser_refs)
    xpad_ref[0:SUBLANES, :] = xpad_ref[TOKEN_TILE:TOKEN_TILE + SUBLANES, :]
    xpad_ref[SUBLANES:SUBLANES + TOKEN_TILE, :] = xb_next
    gate_ref[...] = gate_next


def _ab_in(x2, seq_len, g, w_in, w_v_t, conv_w, conv_b, w_gates, ba, bx, lam):
    t = x2.shape[0]
    n_in = w_in.shape[1]
    c = LRU_WIDTH
    assert seq_len % TOKEN_TILE == 0
    n_tiles = t // TOKEN_TILE
    row = lambda i: (jnp.minimum(i, n_tiles - 1), 0)
    prev_row = lambda i: (jnp.maximum(i - 1, 0), 0)
    kv_per_tile = TOKEN_TILE // KV_TILE
    return pl.pallas_call(
        functools.partial(_ab_in_kernel, seq_len // TOKEN_TILE),
        grid=(n_tiles + 1,),
        in_specs=[pl.BlockSpec((TOKEN_TILE, D_MODEL), row),
                  _resident((1, D_MODEL)),
                  _resident((D_MODEL, n_in)),
                  _resident((DIFF_WIDTH, D_MODEL)),
                  _resident((CONV_WIDTH, c)), _resident((1, c)),
                  _resident((c, 2 * c)), _resident((1, c)), _resident((1, c)),
                  _resident((1, c))],
        out_specs=[pl.BlockSpec((TOKEN_TILE, DIFF_WIDTH), row)] * 2
        + [pl.BlockSpec((kv_per_tile, DIFF_WIDTH, KV_TILE), lambda i: row(i) + (0,))]
        + [pl.BlockSpec((TOKEN_TILE, c), prev_row)],
        out_shape=[jax.ShapeDtypeStruct((t, DIFF_WIDTH), BF16)] * 2
        + [jax.ShapeDtypeStruct((t // KV_TILE, DIFF_WIDTH, KV_TILE), BF16)]
        + [jax.ShapeDtypeStruct((t, c), BF16)],
        scratch_shapes=[pltpu.VMEM((TOKEN_TILE + SUBLANES, c), F32),
                        pltpu.VMEM((TOKEN_TILE, c), F32),
                        pltpu.VMEM((SUBLANES, c), F32)]
        + [pltpu.VMEM((TOKEN_TILE // SUBLANES, SUBLANES, c), F32)] * 2,
        compiler_params=pltpu.CompilerParams(
            dimension_semantics=("arbitrary",), vmem_limit_bytes=VMEM_LIMIT_BYTES),
        name="ab_in_lru",
    )(x2, g, w_in, w_v_t, conv_w, conv_b, w_gates, ba, bx, lam)


def _attn_kernel(lam_init, q_ref, k_ref, vt_ref, lq1_ref, lk1_ref, lq2_ref, lk2_ref,
                 sg_ref, o_ref, m_ref, acc_ref, s_ref):
    first_q = pl.program_id(1) * Q_GROUPS
    d = DIFF_HEAD_DIM
    vd = DIFF_V_DIM
    per_tile = 2 * DIFF_HEADS
    n_chains = Q_GROUPS * per_tile

    m_ref[...] = jnp.full(m_ref.shape, NEG_BIG, F32)
    acc_ref[...] = jnp.zeros(acc_ref.shape, F32)
    ones = jnp.ones((BF16_SUBLANES, KV_TILE), BF16)

    lane = lax.broadcasted_iota(jnp.int32, (Q_TILE, vd), 1)
    qz = []
    for c in range(n_chains):
        t, h, mp = c // per_tile, (c % per_tile) // 2, c % 2
        qh = q_ref[t * Q_TILE:(t + 1) * Q_TILE, h * vd:(h + 1) * vd]
        keep = (lane >= d) if mp else (lane < d)
        qz.append(jnp.where(keep, qh, jnp.zeros_like(qh)))

    def head_of(c):
        return (c % per_tile) // 2

    def scores(c, j):
        h = head_of(c)
        start = pl.multiple_of(j * KV_TILE, KV_TILE)
        kb = k_ref[pl.ds(start, KV_TILE), h * vd:(h + 1) * vd]
        return _dot_nt(kb, qz[c])

    def softmax_update(c, s):
        m_prev = m_ref[c]
        m_new = jnp.maximum(m_prev, jnp.max(s, axis=0, keepdims=True))
        alpha = jnp.exp2(m_prev - m_new)
        p = jnp.exp2(s - m_new)
        m_ref[c] = m_new
        return p.astype(BF16), alpha

    def accumulate(c, j, p, alpha):
        h = head_of(c)
        vb = vt_ref[j, h * vd:(h + 1) * vd, :]
        acc_ref[c] = alpha * acc_ref[c] + _dot(jnp.concatenate([vb, ones], axis=0), p)

    def step(j, chains, masked, prefetch):
        pending = None
        for c in chains:
            s = s_ref[c]
            if c in prefetch:
                s_ref[c] = scores(c, j + 1)
            if c in masked:
                s = jnp.where(diagonal, s, NEG_BIG)
            p, alpha = softmax_update(c, s)
            if pending is not None:
                accumulate(*pending)
            pending = (c, j, p, alpha)
        accumulate(*pending)

    everything = range(n_chains)
    for c in everything:
        s_ref[c] = scores(c, 0)

    def body(j, carry):
        step(j, everything, (), everything)
        return carry

    lax.fori_loop(0, first_q, body, 0)
    kc = lax.broadcasted_iota(jnp.int32, (KV_TILE, Q_TILE), 0) // CHUNK
    qc = lax.broadcasted_iota(jnp.int32, (KV_TILE, Q_TILE), 1) // CHUNK
    diagonal = kc <= qc
    for t in range(Q_GROUPS):
        tile = range(t * per_tile, (t + 1) * per_tile)
        later = range((t + 1) * per_tile, n_chains)
        step(first_q + t, range(t * per_tile, n_chains), tile, later)

    lam = (jnp.exp(jnp.sum(lq1_ref[...] * lk1_ref[...], axis=-1, keepdims=True))
           - jnp.exp(jnp.sum(lq2_ref[...] * lk2_ref[...], axis=-1, keepdims=True))
           + lam_init)
    def normalized(c):
        return acc_ref[c, 0:vd, :] * (1.0 / acc_ref[c, vd:vd + 1, :])

    for c in range(0, n_chains, 2):
        t, h = c // per_tile, head_of(c)
        o = normalized(c) - lam * normalized(c + 1)
        o = o * lax.rsqrt(jnp.mean(o * o, axis=0, keepdims=True) + EPS)
        o = o * (sg_ref[...] * (1.0 - lam_init))
        o_ref[t * Q_TILE:(t + 1) * Q_TILE, h * vd:(h + 1) * vd] = o.T.astype(o_ref.dtype)


def _diff_attention(q, k, vt, lq1, lk1, lq2, lk2, subln_col, lam_init):
    b, s, w = q.shape
    q_rows = Q_GROUPS * Q_TILE
    assert Q_TILE == KV_TILE and Q_TILE % CHUNK == 0 and s % q_rows == 0
    vd = DIFF_V_DIM
    n_maps = Q_GROUPS * 2 * DIFF_HEADS
    vec = lambda n: _resident((1, n))
    return pl.pallas_call(
        functools.partial(_attn_kernel, lam_init),
        grid=(b, s // q_rows),
        in_specs=[pl.BlockSpec((None, q_rows, w), lambda bi, i: (bi, i, 0)),
                  pl.BlockSpec((None, s, w), lambda bi, i: (bi, 0, 0)),
                  pl.BlockSpec((s // KV_TILE, w, KV_TILE), lambda bi, i: (bi, 0, 0)),
                  vec(DIFF_HEAD_DIM), vec(DIFF_HEAD_DIM), vec(DIFF_HEAD_DIM),
                  vec(DIFF_HEAD_DIM), _resident((vd, 1))],
        out_specs=pl.BlockSpec((None, q_rows, w), lambda bi, i: (bi, i, 0)),
        out_shape=jax.ShapeDtypeStruct((b, s, w), BF16),
        scratch_shapes=[pltpu.VMEM((n_maps, 1, Q_TILE), F32),
                        pltpu.VMEM((n_maps, vd + BF16_SUBLANES, Q_TILE), F32),
                        pltpu.VMEM((n_maps, KV_TILE, Q_TILE), F32)],
        compiler_params=pltpu.CompilerParams(
            dimension_semantics=("arbitrary",) * 2, vmem_limit_bytes=VMEM_LIMIT_BYTES),
        name="diff_attention",
    )(q, k, vt, lq1, lk1, lq2, lk2, subln_col)


def _lru_conv_gates(slab, cw_ref, cb_ref, wg_ref, xpad_ref):
    pad = SUBLANES
    rows = LRU_ROWS
    xc = cb_ref[...]
    for j in range(CONV_WIDTH):
        off = slab * rows + pad - (CONV_WIDTH - 1) + j
        xc = xc + cw_ref[j:j + 1, :] * xpad_ref[off:off + rows, :]
    return xc, _dot(xc.astype(BF16), wg_ref[...])


def _lru_rows(slab, xc, gates, ba_ref, bx_ref, lam_ref, aloc_ref, bloc_ref):
    r = _sigmoid(gates[:, :LRU_WIDTH] + ba_ref[...])
    ig = _sigmoid(gates[:, LRU_WIDTH:] + bx_ref[...])
    z = -lam_ref[...]
    softplus = jnp.maximum(z, 0.0) + jnp.log1p(jnp.exp(-jnp.abs(z)))
    log_a = -LRU_C * r * softplus
    a = jnp.exp(log_a)
    bt = jnp.sqrt(-jnp.tanh(log_a) * (a * a + 1.0)) * (ig * xc)

    groups = LRU_ROWS // SUBLANES
    a3 = a.reshape(groups, SUBLANES, LRU_WIDTH)
    b3 = bt.reshape(groups, SUBLANES, LRU_WIDTH)
    sub = lax.broadcasted_iota(jnp.int32, a3.shape, 1)
    shift = 1
    while shift < SUBLANES:
        keep = sub >= shift
        a_prev = jnp.where(keep, pltpu.roll(a3, shift, 1), 1.0)
        b_prev = jnp.where(keep, pltpu.roll(b3, shift, 1), 0.0)
        b3 = a3 * b_prev + b3
        a3 = a3 * a_prev
        shift *= 2
    aloc_ref[slab * groups:(slab + 1) * groups] = a3
    bloc_ref[slab * groups:(slab + 1) * groups] = b3


def _lru_serial(slab, h_in, gate_ref, aloc_ref, bloc_ref, yb_ref):
    groups = LRU_ROWS // SUBLANES
    hs = []
    for g in range(slab * groups, (slab + 1) * groups):
        hg = aloc_ref[g] * h_in + bloc_ref[g]
        hs.append(hg)
        h_in = jnp.broadcast_to(hg[SUBLANES - 1:SUBLANES, :], hg.shape)
    rows = slice(slab * LRU_ROWS, (slab + 1) * LRU_ROWS)
    yb = jax.nn.gelu(gate_ref[rows, :]) * jnp.concatenate(hs, axis=0)
    yb_ref[rows, :] = yb.astype(yb_ref.dtype)
    return h_in


def _block_diag(w):
    nb, d, _ = w.shape
    eye = jnp.eye(nb, dtype=w.dtype)
    return (eye[:, None, :, None] * w[:, :, None, :]).reshape(nb * d, nb * d)


def _ffn(x, g, w1_ref, w2_ref):
    h = _rms(x, g).astype(BF16)
    acc = None
    for c in range(D_FF // FF_CHUNK):
        cols = slice(c * FF_CHUNK, (c + 1) * FF_CHUNK)
        t = jnp.maximum(_dot(h, w1_ref[:, cols]), 0.0)
        part = _dot((t * t).astype(BF16), w2_ref[cols, :])
        acc = part if acc is None else acc + part
    return x + acc


def _finish(x, gfin_ref, o_ref):
    if gfin_ref is not None:
        x = _rms(x, gfin_ref[...])
    o_ref[...] = x


def _ab_out_ffn_kernel(final, x_ref, ya_ref, yb_ref, wo_ref, gf_ref, w1_ref, w2_ref,
                       *rest):
    gfin_ref, o_ref = rest if final else (None, rest[0])
    y = jnp.concatenate([ya_ref[...], yb_ref[...]], axis=1)
    x = x_ref[...] + _dot(y, wo_ref[...])
    _finish(_ffn(x, gf_ref[...], w1_ref, w2_ref), gfin_ref, o_ref)


def _ab_out_ffn(x2, ya, yb, w_out, g_ffn, w1, w2, g_final):
    t = x2.shape[0]
    row = lambda i: (i, 0)
    final = g_final is not None
    in_specs = [pl.BlockSpec((MLP_TILE, D_MODEL), row),
                pl.BlockSpec((MLP_TILE, DIFF_WIDTH), row),
                pl.BlockSpec((MLP_TILE, LRU_WIDTH), row),
                _resident(w_out.shape), _resident((1, D_MODEL)),
                _resident(w1.shape), _resident(w2.shape)]
    args = [x2, ya, yb, w_out, g_ffn, w1, w2]
    if final:
        in_specs.append(_resident((1, D_MODEL)))
        args.append(g_final)
    return pl.pallas_call(
        functools.partial(_ab_out_ffn_kernel, final),
        grid=(t // MLP_TILE,),
        in_specs=in_specs,
        out_specs=pl.BlockSpec((MLP_TILE, D_MODEL), row),
        out_shape=jax.ShapeDtypeStruct((t, D_MODEL), F32),
        compiler_params=pltpu.CompilerParams(
            dimension_semantics=("arbitrary",), vmem_limit_bytes=VMEM_LIMIT_BYTES),
        name="ab_out_ffn",
    )(*args)


def _sgu_ffn_kernel(final, x_ref, gm_ref, win_ref, lng_ref, lnb_ref, ws_ref, bs_ref,
                    wout_ref, gf_ref, w1_ref, w2_ref, *rest):
    if final:
        gfin_ref, o_ref, gated_ref = rest
    else:
        gfin_ref, (o_ref, gated_ref) = None, rest
    x = x_ref[...]
    h = _rms(x, gm_ref[...]).astype(BF16)
    u_cols = slice(0, SGU_WIDTH)
    v_cols = slice(SGU_WIDTH, 2 * SGU_WIDTH)
    v = jax.nn.gelu(_dot(h, win_ref[:, v_cols]))
    mu = jnp.mean(v, axis=-1, keepdims=True)
    vc = v - mu
    var = jnp.mean(vc * vc, axis=-1, keepdims=True)
    vn = (vc * lax.rsqrt(var + EPS) * lng_ref[...] + lnb_ref[...]).astype(BF16)
    u = jax.nn.gelu(_dot(h, win_ref[:, u_cols]))

    pc = lax.broadcasted_iota(jnp.int32, (SGU_BLOCK, SGU_BLOCK), 0) // CHUNK
    qc = lax.broadcasted_iota(jnp.int32, (SGU_BLOCK, SGU_BLOCK), 1) // CHUNK
    mask = qc <= pc
    for g in range(SGU_GROUPS):
        cols = slice(g * SGU_GROUP_DIM, (g + 1) * SGU_GROUP_DIM)
        w = jnp.where(mask, ws_ref[g], 0.0).astype(BF16)
        bias = bs_ref[:, g:g + 1]
        for n in range(TOKEN_TILE // SGU_BLOCK):
            rows = slice(n * SGU_BLOCK, (n + 1) * SGU_BLOCK)
            mixed = _dot(w, vn[rows, cols]) + bias
            gated_ref[rows, cols] = (u[rows, cols] * mixed).astype(BF16)

    x = x + _dot(gated_ref[...], wout_ref[...])
    _finish(_ffn(x, gf_ref[...], w1_ref, w2_ref), gfin_ref, o_ref)


def _sgu_ffn(x2, g_mix, w_in, ln_g, ln_b, w_s, b_s_t, w_out, g_ffn, w1, w2, g_final):
    t = x2.shape[0]
    row = lambda i: (i, 0)
    final = g_final is not None
    in_specs = [pl.BlockSpec((TOKEN_TILE, D_MODEL), row),
                _resident((1, D_MODEL)), _resident(w_in.shape),
                _resident((1, SGU_WIDTH)), _resident((1, SGU_WIDTH)),
                _resident(w_s.shape), _resident(b_s_t.shape),
                _resident(w_out.shape), _resident((1, D_MODEL)),
                _resident(w1.shape), _resident(w2.shape)]
    args = [x2, g_mix, w_in, ln_g, ln_b, w_s, b_s_t, w_out, g_ffn, w1, w2]
    if final:
        in_specs.append(_resident((1, D_MODEL)))
        args.append(g_final)
    return pl.pallas_call(
        functools.partial(_sgu_ffn_kernel, final),
        grid=(t // TOKEN_TILE,),
        in_specs=in_specs,
        out_specs=pl.BlockSpec((TOKEN_TILE, D_MODEL), row),
        out_shape=jax.ShapeDtypeStruct((t, D_MODEL), F32),
        scratch_shapes=[pltpu.VMEM((TOKEN_TILE, SGU_WIDTH), BF16)],
        compiler_params=pltpu.CompilerParams(
            dimension_semantics=("arbitrary",), vmem_limit_bytes=VMEM_LIMIT_BYTES),
        name="sgu_ffn",
    )(*args)


def kernel(x, norm_mix, norm_ffn, norm_final, ab_w_in, diff_lq1, diff_lk1, diff_lq2,
           diff_lk2, diff_subln, lru_conv_w, lru_conv_b, lru_wa, lru_ba, lru_wx, lru_bx,
           lru_lambda, ab_w_out, c_w_in, c_ln_g, c_ln_b, c_w_s, c_b_s, c_w_out, ffn_w1,
           ffn_w2):
    b, s, d = x.shape
    t = b * s
    assert t % TOKEN_TILE == 0 and TOKEN_TILE % SGU_BLOCK == 0
    x2 = x.reshape(t, d)
    vec = lambda a: a.reshape(1, -1)
    for layer in range(DEPTH):
        g_final = vec(norm_final) if layer == DEPTH - 1 else None
        g_ffn = vec(norm_ffn[layer])
        w1 = ffn_w1[layer].astype(BF16)
        w2 = ffn_w2[layer].astype(BF16)
        if layer % 2 == 0:
            e = layer // 2
            lam_init = 0.8 - 0.6 * math.exp(-0.3 * layer)
            w_in = ab_w_in[e].astype(BF16)
            w_v_t = w_in[:, 2 * DIFF_WIDTH:3 * DIFF_WIDTH].T
            w_gates = jnp.concatenate(
                [_block_diag(lru_wa[e]), _block_diag(lru_wx[e])], axis=1).astype(BF16)
            q, k, vt, yb = _ab_in(x2, s, vec(norm_mix[layer]), w_in, w_v_t, lru_conv_w[e],
                                  vec(lru_conv_b[e]), w_gates, vec(lru_ba[e]),
                                  vec(lru_bx[e]), vec(lru_lambda[e]))
            seq = lambda a: a.reshape(b, s, a.shape[-1])
            ya = _diff_attention(seq(q), seq(k), vt, vec(diff_lq1[e]), vec(diff_lk1[e]),
                                 vec(diff_lq2[e]), vec(diff_lk2[e]),
                                 diff_subln[e].reshape(-1, 1), lam_init)
            x2 = _ab_out_ffn(x2, ya.reshape(t, -1), yb, ab_w_out[e].astype(BF16), g_ffn,
                             w1, w2, g_final)
        else:
            o = layer // 2
            x2 = _sgu_ffn(x2, vec(norm_mix[layer]), c_w_in[o].astype(BF16), vec(c_ln_g[o]),
                          vec(c_ln_b[o]), c_w_s[o], c_b_s[o].T, c_w_out[o].astype(BF16),
                          g_ffn, w1, w2, g_final)
    return x2.reshape(b, s, d)
```

```python
import functools
import math

import jax
import jax.numpy as jnp
from jax import lax
from jax.experimental import pallas as pl
from jax.experimental.pallas import tpu as pltpu

D_MODEL = 1024
DEPTH = 4
CHUNK = 64
DIFF_HEADS = 4
DIFF_HEAD_DIM = 64
DIFF_V_DIM = 2 * DIFF_HEAD_DIM
DIFF_WIDTH = DIFF_HEADS * DIFF_V_DIM
LRU_WIDTH = 512
LRU_BLOCKS = 8
LRU_BLOCK_DIM = LRU_WIDTH // LRU_BLOCKS
LRU_C = 8.0
CONV_WIDTH = 4
SGU_WIDTH = D_MODEL
SGU_GROUPS = 8
SGU_GROUP_DIM = SGU_WIDTH // SGU_GROUPS
SGU_BLOCK = 128
D_FF = 4 * D_MODEL
EPS = 1e-6

F32 = jnp.float32
BF16 = jnp.bfloat16

VMEM_LIMIT_BYTES = 56 * 1024 * 1024
SUBLANES = 8
BF16_SUBLANES = 16
LOG2_E = math.log2(math.e)

TOKEN_TILE = 512
MLP_TILE = 1024
FF_CHUNK = 1024
Q_TILE = 256
Q_GROUPS = 4
KV_TILE = 256
LRU_ROWS = 128
NEG_BIG = -1e30


def _resident(shape):
    return pl.BlockSpec(shape, lambda *_: (0,) * len(shape),
                        pipeline_mode=pl.Buffered(1))


def _resident_layer(stacked, layer):
    shape = stacked.shape[1:]
    return pl.BlockSpec((None,) + shape, lambda *_: (layer,) + (0,) * len(shape),
                        pipeline_mode=pl.Buffered(1))


def _rms(x, g):
    return x * lax.rsqrt(jnp.mean(x * x, axis=-1, keepdims=True) + EPS) * g


def _sigmoid(x):
    return 0.5 * jnp.tanh(0.5 * x) + 0.5


def _dot(a, b):
    return jnp.dot(a, b, preferred_element_type=F32)


def _dot_nt(a, b):
    return lax.dot_general(a, b, (((1,), (1,)), ((), ())), preferred_element_type=F32)


def _ab_in_kernel(tiles_per_seq, x_ref, g_ref, w_ref, wvt_ref, cw_ref, cb_ref, wg_ref,
                  ba_ref, bx_ref, lam_ref, q_ref, k_ref, vt_ref, yb_ref, *lru_scratch):
    i = pl.program_id(0)
    xpad_ref, gate_ref, hstate_ref, aloc_ref, bloc_ref = lru_scratch

    @pl.when(i == 0)
    def _():
        xpad_ref[...] = jnp.zeros(xpad_ref.shape, F32)
        gate_ref[...] = jnp.zeros(gate_ref.shape, F32)
        hstate_ref[...] = jnp.zeros(hstate_ref.shape, F32)

    @pl.when(lax.rem(i - 1, tiles_per_seq) == 0)
    def _():
        xpad_ref[0:SUBLANES, :] = jnp.zeros((SUBLANES, LRU_WIDTH), F32)
        hstate_ref[...] = jnp.zeros(hstate_ref.shape, F32)

    h = _rms(x_ref[...], g_ref[...]).astype(BF16)
    w = DIFF_WIDTH
    q_scale = DIFF_HEAD_DIM ** -0.5 * LOG2_E
    conv_refs = (cw_ref, cb_ref, wg_ref, xpad_ref)
    row_refs = (ba_ref, bx_ref, lam_ref, aloc_ref, bloc_ref)
    assert TOKEN_TILE // LRU_ROWS == 4

    def lru_rows(slab):
        xc, gates = _lru_conv_gates(slab, *conv_refs)
        _lru_rows(slab, xc, gates, *row_refs)

    ser_refs = (gate_ref, aloc_ref, bloc_ref, yb_ref)
    assert TOKEN_TILE // KV_TILE == 2

    def v_transposed(c):
        toks = slice(c * KV_TILE, (c + 1) * KV_TILE)
        vt_ref[c] = _dot_nt(wvt_ref[...], h[toks]).astype(BF16)

    state = hstate_ref[...]
    lru_rows(0)
    q_ref[...] = (_dot(h, w_ref[:, 0:w]) * q_scale).astype(BF16)
    v_transposed(0)
    state = _lru_serial(0, state, *ser_refs)
    lru_rows(1)
    k_ref[...] = _dot(h, w_ref[:, w:2 * w]).astype(BF16)
    v_transposed(1)
    state = _lru_serial(1, state, *ser_refs)
    lru_rows(2)
    xb_next = _dot(h, w_ref[:, 3 * w:3 * w + LRU_WIDTH])
    state = _lru_serial(2, state, *ser_refs)
    lru_rows(3)
    gate_next = _dot(h, w_ref[:, 3 * w + LRU_WIDTH:])
    hstate_ref[...] = _lru_serial(3, state, *ser_refs)
    xpad_ref[0:SUBLANES, :] = xpad_ref[TOKEN_TILE:TOKEN_TILE + SUBLANES, :]
    xpad_ref[SUBLANES:SUBLANES + TOKEN_TILE, :] = xb_next
    gate_ref[...] = gate_next


def _ab_in(x2, seq_len, g, w_in, e, w_v_t, conv_w, conv_b, w_gates, ba, bx, lam):
    t = x2.shape[0]
    c = LRU_WIDTH
    assert seq_len % TOKEN_TILE == 0
    n_tiles = t // TOKEN_TILE
    row = lambda i: (jnp.minimum(i, n_tiles - 1), 0)
    prev_row = lambda i: (jnp.maximum(i - 1, 0), 0)
    kv_per_tile = TOKEN_TILE // KV_TILE
    return pl.pallas_call(
        functools.partial(_ab_in_kernel, seq_len // TOKEN_TILE),
        grid=(n_tiles + 1,),
        in_specs=[pl.BlockSpec((TOKEN_TILE, D_MODEL), row),
                  _resident((1, D_MODEL)),
                  _resident_layer(w_in, e),
                  _resident((DIFF_WIDTH, D_MODEL)),
                  _resident((CONV_WIDTH, c)), _resident((1, c)),
                  _resident((c, 2 * c)), _resident((1, c)), _resident((1, c)),
                  _resident((1, c))],
        out_specs=[pl.BlockSpec((TOKEN_TILE, DIFF_WIDTH), row)] * 2
        + [pl.BlockSpec((kv_per_tile, DIFF_WIDTH, KV_TILE), lambda i: row(i) + (0,))]
        + [pl.BlockSpec((TOKEN_TILE, c), prev_row)],
        out_shape=[jax.ShapeDtypeStruct((t, DIFF_WIDTH), BF16)] * 2
        + [jax.ShapeDtypeStruct((t // KV_TILE, DIFF_WIDTH, KV_TILE), BF16)]
        + [jax.ShapeDtypeStruct((t, c), BF16)],
        scratch_shapes=[pltpu.VMEM((TOKEN_TILE + SUBLANES, c), F32),
                        pltpu.VMEM((TOKEN_TILE, c), F32),
                        pltpu.VMEM((SUBLANES, c), F32)]
        + [pltpu.VMEM((TOKEN_TILE // SUBLANES, SUBLANES, c), F32)] * 2,
        compiler_params=pltpu.CompilerParams(
            dimension_semantics=("arbitrary",), vmem_limit_bytes=VMEM_LIMIT_BYTES),
        name="ab_in_lru",
    )(x2, g, w_in, w_v_t, conv_w, conv_b, w_gates, ba, bx, lam)


def _attn_kernel(lam_init, q_ref, k_ref, vt_ref, lq1_ref, lk1_ref, lq2_ref, lk2_ref,
                 sg_ref, o_ref, m_ref, acc_ref, s_ref):
    first_q = pl.program_id(1) * Q_GROUPS
    d = DIFF_HEAD_DIM
    vd = DIFF_V_DIM
    per_tile = 2 * DIFF_HEADS
    n_chains = Q_GROUPS * per_tile

    m_ref[...] = jnp.full(m_ref.shape, NEG_BIG, F32)
    acc_ref[...] = jnp.zeros(acc_ref.shape, F32)
    ones = jnp.ones((BF16_SUBLANES, KV_TILE), BF16)

    lane = lax.broadcasted_iota(jnp.int32, (Q_TILE, vd), 1)
    qz = []
    for c in range(n_chains):
        t, h, mp = c // per_tile, (c % per_tile) // 2, c % 2
        qh = q_ref[t * Q_TILE:(t + 1) * Q_TILE, h * vd:(h + 1) * vd]
        keep = (lane >= d) if mp else (lane < d)
        qz.append(jnp.where(keep, qh, jnp.zeros_like(qh)))

    def head_of(c):
        return (c % per_tile) // 2

    def scores(c, j):
        h = head_of(c)
        start = pl.multiple_of(j * KV_TILE, KV_TILE)
        kb = k_ref[pl.ds(start, KV_TILE), h * vd:(h + 1) * vd]
        return _dot_nt(kb, qz[c])

    def softmax_update(c, s):
        m_prev = m_ref[c]
        m_new = jnp.maximum(m_prev, jnp.max(s, axis=0, keepdims=True))
        alpha = jnp.exp2(m_prev - m_new)
        p = jnp.exp2(s - m_new)
        m_ref[c] = m_new
        return p.astype(BF16), alpha

    def accumulate(c, j, p, alpha):
        h = head_of(c)
        vb = vt_ref[j, h * vd:(h + 1) * vd, :]
        acc_ref[c] = alpha * acc_ref[c] + _dot(jnp.concatenate([vb, ones], axis=0), p)

    def step(j, chains, masked, prefetch):
        pending = None
        for c in chains:
            s = s_ref[c]
            if c in prefetch:
                s_ref[c] = scores(c, j + 1)
            if c in masked:
                s = jnp.where(diagonal, s, NEG_BIG)
            p, alpha = softmax_update(c, s)
            if pending is not None:
                accumulate(*pending)
            pending = (c, j, p, alpha)
        accumulate(*pending)

    everything = range(n_chains)
    for c in everything:
        s_ref[c] = scores(c, 0)

    def body(j, carry):
        step(j, everything, (), everything)
        return carry

    lax.fori_loop(0, first_q, body, 0)
    kc = lax.broadcasted_iota(jnp.int32, (KV_TILE, Q_TILE), 0) // CHUNK
    qc = lax.broadcasted_iota(jnp.int32, (KV_TILE, Q_TILE), 1) // CHUNK
    diagonal = kc <= qc
    for t in range(Q_GROUPS):
        tile = range(t * per_tile, (t + 1) * per_tile)
        later = range((t + 1) * per_tile, n_chains)
        step(first_q + t, range(t * per_tile, n_chains), tile, later)

    lam = (jnp.exp(jnp.sum(lq1_ref[...] * lk1_ref[...], axis=-1, keepdims=True))
           - jnp.exp(jnp.sum(lq2_ref[...] * lk2_ref[...], axis=-1, keepdims=True))
           + lam_init)
    def normalized(c):
        return acc_ref[c, 0:vd, :] * (1.0 / acc_ref[c, vd:vd + 1, :])

    for c in range(0, n_chains, 2):
        t, h = c // per_tile, head_of(c)
        o = normalized(c) - lam * normalized(c + 1)
        o = o * lax.rsqrt(jnp.mean(o * o, axis=0, keepdims=True) + EPS)
        o = o * (sg_ref[...] * (1.0 - lam_init))
        o_ref[t * Q_TILE:(t + 1) * Q_TILE, h * vd:(h + 1) * vd] = o.T.astype(o_ref.dtype)


def _diff_attention(q, k, vt, lq1, lk1, lq2, lk2, subln_col, lam_init):
    b, s, w = q.shape
    q_rows = Q_GROUPS * Q_TILE
    assert Q_TILE == KV_TILE and Q_TILE % CHUNK == 0 and s % q_rows == 0
    vd = DIFF_V_DIM
    n_maps = Q_GROUPS * 2 * DIFF_HEADS
    vec = lambda n: _resident((1, n))
    return pl.pallas_call(
        functools.partial(_attn_kernel, lam_init),
        grid=(b, s // q_rows),
        in_specs=[pl.BlockSpec((None, q_rows, w), lambda bi, i: (bi, i, 0)),
                  pl.BlockSpec((None, s, w), lambda bi, i: (bi, 0, 0)),
                  pl.BlockSpec((s // KV_TILE, w, KV_TILE), lambda bi, i: (bi, 0, 0)),
                  vec(DIFF_HEAD_DIM), vec(DIFF_HEAD_DIM), vec(DIFF_HEAD_DIM),
                  vec(DIFF_HEAD_DIM), _resident((vd, 1))],
        out_specs=pl.BlockSpec((None, q_rows, w), lambda bi, i: (bi, i, 0)),
        out_shape=jax.ShapeDtypeStruct((b, s, w), BF16),
        scratch_shapes=[pltpu.VMEM((n_maps, 1, Q_TILE), F32),
                        pltpu.VMEM((n_maps, vd + BF16_SUBLANES, Q_TILE), F32),
                        pltpu.VMEM((n_maps, KV_TILE, Q_TILE), F32)],
        compiler_params=pltpu.CompilerParams(
            dimension_semantics=("arbitrary",) * 2, vmem_limit_bytes=VMEM_LIMIT_BYTES),
        name="diff_attention",
    )(q, k, vt, lq1, lk1, lq2, lk2, subln_col)


def _lru_conv_gates(slab, cw_ref, cb_ref, wg_ref, xpad_ref):
    pad = SUBLANES
    rows = LRU_ROWS
    xc = cb_ref[...]
    for j in range(CONV_WIDTH):
        off = slab * rows + pad - (CONV_WIDTH - 1) + j
        xc = xc + cw_ref[j:j + 1, :] * xpad_ref[off:off + rows, :]
    return xc, _dot(xc.astype(BF16), wg_ref[...])


def _lru_rows(slab, xc, gates, ba_ref, bx_ref, lam_ref, aloc_ref, bloc_ref):
    r = _sigmoid(gates[:, :LRU_WIDTH] + ba_ref[...])
    ig = _sigmoid(gates[:, LRU_WIDTH:] + bx_ref[...])
    z = -lam_ref[...]
    softplus = jnp.maximum(z, 0.0) + jnp.log1p(jnp.exp(-jnp.abs(z)))
    log_a = -LRU_C * r * softplus
    a = jnp.exp(log_a)
    bt = jnp.sqrt(-jnp.tanh(log_a) * (a * a + 1.0)) * (ig * xc)

    groups = LRU_ROWS // SUBLANES
    a3 = a.reshape(groups, SUBLANES, LRU_WIDTH)
    b3 = bt.reshape(groups, SUBLANES, LRU_WIDTH)
    sub = lax.broadcasted_iota(jnp.int32, a3.shape, 1)
    shift = 1
    while shift < SUBLANES:
        keep = sub >= shift
        a_prev = jnp.where(keep, pltpu.roll(a3, shift, 1), 1.0)
        b_prev = jnp.where(keep, pltpu.roll(b3, shift, 1), 0.0)
        b3 = a3 * b_prev + b3
        a3 = a3 * a_prev
        shift *= 2
    aloc_ref[slab * groups:(slab + 1) * groups] = a3
    bloc_ref[slab * groups:(slab + 1) * groups] = b3


def _lru_serial(slab, h_in, gate_ref, aloc_ref, bloc_ref, yb_ref):
    groups = LRU_ROWS // SUBLANES
    hs = []
    for g in range(slab * groups, (slab + 1) * groups):
        hg = aloc_ref[g] * h_in + bloc_ref[g]
        hs.append(hg)
        h_in = jnp.broadcast_to(hg[SUBLANES - 1:SUBLANES, :], hg.shape)
    rows = slice(slab * LRU_ROWS, (slab + 1) * LRU_ROWS)
    yb = jax.nn.gelu(gate_ref[rows, :]) * jnp.concatenate(hs, axis=0)
    yb_ref[rows, :] = yb.astype(yb_ref.dtype)
    return h_in


def _block_diag(w):
    nb, d, _ = w.shape
    eye = jnp.eye(nb, dtype=w.dtype)
    return (eye[:, None, :, None] * w[:, :, None, :]).reshape(nb * d, nb * d)


def _ffn(x, g, w1_ref, w2_ref):
    h = _rms(x, g).astype(BF16)
    acc = None
    for c in range(D_FF // FF_CHUNK):
        cols = slice(c * FF_CHUNK, (c + 1) * FF_CHUNK)
        t = jnp.maximum(_dot(h, w1_ref[:, cols]), 0.0)
        part = _dot((t * t).astype(BF16), w2_ref[cols, :])
        acc = part if acc is None else acc + part
    return x + acc


def _finish(x, gfin_ref, o_ref):
    if gfin_ref is not None:
        x = _rms(x, gfin_ref[...])
    o_ref[...] = x


def _ab_out_ffn_kernel(final, x_ref, ya_ref, yb_ref, wo_ref, gf_ref, w1_ref, w2_ref,
                       *rest):
    gfin_ref, o_ref = rest if final else (None, rest[0])
    y = jnp.concatenate([ya_ref[...], yb_ref[...]], axis=1)
    x = x_ref[...] + _dot(y, wo_ref[...])
    _finish(_ffn(x, gf_ref[...], w1_ref, w2_ref), gfin_ref, o_ref)


def _ab_out_ffn(x2, ya, yb, w_out, e, g_ffn, w1, w2, layer, g_final):
    t = x2.shape[0]
    row = lambda i: (i, 0)
    final = g_final is not None
    in_specs = [pl.BlockSpec((MLP_TILE, D_MODEL), row),
                pl.BlockSpec((MLP_TILE, DIFF_WIDTH), row),
                pl.BlockSpec((MLP_TILE, LRU_WIDTH), row),
                _resident_layer(w_out, e), _resident((1, D_MODEL)),
                _resident_layer(w1, layer), _resident_layer(w2, layer)]
    args = [x2, ya, yb, w_out, g_ffn, w1, w2]
    if final:
        in_specs.append(_resident((1, D_MODEL)))
        args.append(g_final)
    return pl.pallas_call(
        functools.partial(_ab_out_ffn_kernel, final),
        grid=(t // MLP_TILE,),
        in_specs=in_specs,
        out_specs=pl.BlockSpec((MLP_TILE, D_MODEL), row),
        out_shape=jax.ShapeDtypeStruct((t, D_MODEL), F32),
        compiler_params=pltpu.CompilerParams(
            dimension_semantics=("arbitrary",), vmem_limit_bytes=VMEM_LIMIT_BYTES),
        name="ab_out_ffn",
    )(*args)


def _sgu_ffn_kernel(final, x_ref, gm_ref, win_ref, lng_ref, lnb_ref, ws_ref, bs_ref,
                    wout_ref, gf_ref, w1_ref, w2_ref, *rest):
    if final:
        gfin_ref, o_ref, gated_ref = rest
    else:
        gfin_ref, (o_ref, gated_ref) = None, rest
    x = x_ref[...]
    h = _rms(x, gm_ref[...]).astype(BF16)
    u_cols = slice(0, SGU_WIDTH)
    v_cols = slice(SGU_WIDTH, 2 * SGU_WIDTH)
    v = jax.nn.gelu(_dot(h, win_ref[:, v_cols]))
    mu = jnp.mean(v, axis=-1, keepdims=True)
    vc = v - mu
    var = jnp.mean(vc * vc, axis=-1, keepdims=True)
    vn = (vc * lax.rsqrt(var + EPS) * lng_ref[...] + lnb_ref[...]).astype(BF16)
    u = jax.nn.gelu(_dot(h, win_ref[:, u_cols]))

    pc = lax.broadcasted_iota(jnp.int32, (SGU_BLOCK, SGU_BLOCK), 0) // CHUNK
    qc = lax.broadcasted_iota(jnp.int32, (SGU_BLOCK, SGU_BLOCK), 1) // CHUNK
    mask = qc <= pc
    for g in range(SGU_GROUPS):
        cols = slice(g * SGU_GROUP_DIM, (g + 1) * SGU_GROUP_DIM)
        w = jnp.where(mask, ws_ref[g], 0.0).astype(BF16)
        bias = bs_ref[:, g:g + 1]
        for n in range(TOKEN_TILE // SGU_BLOCK):
            rows = slice(n * SGU_BLOCK, (n + 1) * SGU_BLOCK)
            mixed = _dot(w, vn[rows, cols]) + bias
            gated_ref[rows, cols] = (u[rows, cols] * mixed).astype(BF16)

    x = x + _dot(gated_ref[...], wout_ref[...])
    _finish(_ffn(x, gf_ref[...], w1_ref, w2_ref), gfin_ref, o_ref)


def _sgu_ffn(x2, g_mix, w_in, ln_g, ln_b, w_s, b_s_t, w_out, o, g_ffn, w1, w2, layer,
             g_final):
    t = x2.shape[0]
    row = lambda i: (i, 0)
    final = g_final is not None
    in_specs = [pl.BlockSpec((TOKEN_TILE, D_MODEL), row),
                _resident((1, D_MODEL)), _resident_layer(w_in, o),
                _resident((1, SGU_WIDTH)), _resident((1, SGU_WIDTH)),
                _resident_layer(w_s, o), _resident(b_s_t.shape),
                _resident_layer(w_out, o), _resident((1, D_MODEL)),
                _resident_layer(w1, layer), _resident_layer(w2, layer)]
    args = [x2, g_mix, w_in, ln_g, ln_b, w_s, b_s_t, w_out, g_ffn, w1, w2]
    if final:
        in_specs.append(_resident((1, D_MODEL)))
        args.append(g_final)
    return pl.pallas_call(
        functools.partial(_sgu_ffn_kernel, final),
        grid=(t // TOKEN_TILE,),
        in_specs=in_specs,
        out_specs=pl.BlockSpec((TOKEN_TILE, D_MODEL), row),
        out_shape=jax.ShapeDtypeStruct((t, D_MODEL), F32),
        scratch_shapes=[pltpu.VMEM((TOKEN_TILE, SGU_WIDTH), BF16)],
        compiler_params=pltpu.CompilerParams(
            dimension_semantics=("arbitrary",), vmem_limit_bytes=VMEM_LIMIT_BYTES),
        name="sgu_ffn",
    )(*args)


def kernel(x, norm_mix, norm_ffn, norm_final, ab_w_in, diff_lq1, diff_lk1, diff_lq2,
           diff_lk2, diff_subln, lru_conv_w, lru_conv_b, lru_wa, lru_ba, lru_wx, lru_bx,
           lru_lambda, ab_w_out, c_w_in, c_ln_g, c_ln_b, c_w_s, c_b_s, c_w_out, ffn_w1,
           ffn_w2):
    b, s, d = x.shape
    t = b * s
    assert t % TOKEN_TILE == 0 and TOKEN_TILE % SGU_BLOCK == 0
    x2 = x.reshape(t, d)
    vec = lambda a: a.reshape(1, -1)
    w1, w2 = ffn_w1.astype(BF16), ffn_w2.astype(BF16)
    w_in_ab, w_out_ab = ab_w_in.astype(BF16), ab_w_out.astype(BF16)
    w_in_c, w_out_c = c_w_in.astype(BF16), c_w_out.astype(BF16)
    for layer in range(DEPTH):
        g_final = vec(norm_final) if layer == DEPTH - 1 else None
        g_ffn = vec(norm_ffn[layer])
        if layer % 2 == 0:
            e = layer // 2
            lam_init = 0.8 - 0.6 * math.exp(-0.3 * layer)
            w_v_t = w_in_ab[e, :, 2 * DIFF_WIDTH:3 * DIFF_WIDTH].T
            w_gates = jnp.concatenate(
                [_block_diag(lru_wa[e]), _block_diag(lru_wx[e])], axis=1).astype(BF16)
            q, k, vt, yb = _ab_in(x2, s, vec(norm_mix[layer]), w_in_ab, e, w_v_t,
                                  lru_conv_w[e], vec(lru_conv_b[e]), w_gates,
                                  vec(lru_ba[e]), vec(lru_bx[e]), vec(lru_lambda[e]))
            seq = lambda a: a.reshape(b, s, a.shape[-1])
            ya = _diff_attention(seq(q), seq(k), vt, vec(diff_lq1[e]), vec(diff_lk1[e]),
                                 vec(diff_lq2[e]), vec(diff_lk2[e]),
                                 diff_subln[e].reshape(-1, 1), lam_init)
            x2 = _ab_out_ffn(x2, ya.reshape(t, -1), yb, w_out_ab, e, g_ffn, w1, w2, layer,
                             g_final)
        else:
            o = layer // 2
            x2 = _sgu_ffn(x2, vec(norm_mix[layer]), w_in_c, vec(c_ln_g[o]), vec(c_ln_b[o]),
                          c_w_s, c_b_s[o].T, w_out_c, o, g_ffn, w1, w2, layer, g_final)
    return x2.reshape(b, s, d)
```

```python
import functools
import math

import jax
import jax.numpy as jnp
from jax import lax
from jax.experimental import pallas as pl
from jax.experimental.pallas import tpu as pltpu

D_MODEL = 1024
DEPTH = 4
CHUNK = 64
DIFF_HEADS = 4
DIFF_HEAD_DIM = 64
DIFF_V_DIM = 2 * DIFF_HEAD_DIM
DIFF_WIDTH = DIFF_HEADS * DIFF_V_DIM
LRU_WIDTH = 512
LRU_BLOCKS = 8
LRU_BLOCK_DIM = LRU_WIDTH // LRU_BLOCKS
LRU_C = 8.0
CONV_WIDTH = 4
SGU_WIDTH = D_MODEL
SGU_GROUPS = 8
SGU_GROUP_DIM = SGU_WIDTH // SGU_GROUPS
SGU_BLOCK = 128
D_FF = 4 * D_MODEL
EPS = 1e-6

F32 = jnp.float32
BF16 = jnp.bfloat16

VMEM_LIMIT_BYTES = 56 * 1024 * 1024
SUBLANES = 8
BF16_SUBLANES = 16
LOG2_E = math.log2(math.e)

TOKEN_TILE = 512
MLP_TILE = 1024
SGU_TILE = 1024
FF_CHUNK = 1024
Q_TILE = 256
Q_GROUPS = 4
KV_TILE = 256
LRU_ROWS = 128
NEG_BIG = -1e30


def _resident(shape):
    return pl.BlockSpec(shape, lambda *_: (0,) * len(shape),
                        pipeline_mode=pl.Buffered(1))


def _resident_layer(stacked, layer):
    shape = stacked.shape[1:]
    return pl.BlockSpec((None,) + shape, lambda *_: (layer,) + (0,) * len(shape),
                        pipeline_mode=pl.Buffered(1))


def _rms(x, g):
    return x * lax.rsqrt(jnp.mean(x * x, axis=-1, keepdims=True) + EPS) * g


def _sigmoid(x):
    return 0.5 * jnp.tanh(0.5 * x) + 0.5


def _dot(a, b):
    return jnp.dot(a, b, preferred_element_type=F32)


def _dot_nt(a, b):
    return lax.dot_general(a, b, (((1,), (1,)), ((), ())), preferred_element_type=F32)


def _ab_in_kernel(tiles_per_seq, x_ref, g_ref, w_ref, cw_ref, cb_ref, wg_ref,
                  ba_ref, bx_ref, lam_ref, q_ref, k_ref, vt_ref, yb_ref, wvt_ref,
                  *lru_scratch):
    i = pl.program_id(0)
    xpad_ref, gate_ref, hstate_ref, aloc_ref, bloc_ref = lru_scratch
    w = DIFF_WIDTH

    @pl.when(i == 0)
    def _():
        xpad_ref[...] = jnp.zeros(xpad_ref.shape, F32)
        gate_ref[...] = jnp.zeros(gate_ref.shape, F32)
        hstate_ref[...] = jnp.zeros(hstate_ref.shape, F32)
        wvt_ref[...] = w_ref[:, 2 * w:3 * w].T

    @pl.when(lax.rem(i - 1, tiles_per_seq) == 0)
    def _():
        xpad_ref[0:SUBLANES, :] = jnp.zeros((SUBLANES, LRU_WIDTH), F32)
        hstate_ref[...] = jnp.zeros(hstate_ref.shape, F32)

    h = _rms(x_ref[...], g_ref[...]).astype(BF16)
    q_scale = DIFF_HEAD_DIM ** -0.5 * LOG2_E
    conv_refs = (cw_ref, cb_ref, wg_ref, xpad_ref)
    row_refs = (ba_ref, bx_ref, lam_ref, aloc_ref, bloc_ref)
    ser_refs = (gate_ref, aloc_ref, bloc_ref, yb_ref)
    staged = {}

    def proj_q():
        q_ref[...] = (_dot(h, w_ref[:, 0:w]) * q_scale).astype(BF16)

    def proj_k():
        k_ref[...] = _dot(h, w_ref[:, w:2 * w]).astype(BF16)

    def proj_xb():
        staged["xb"] = _dot(h, w_ref[:, 3 * w:3 * w + LRU_WIDTH])

    def proj_gate():
        staged["gate"] = _dot(h, w_ref[:, 3 * w + LRU_WIDTH:])

    def proj_vt(c):
        toks = slice(c * KV_TILE, (c + 1) * KV_TILE)
        vt_ref[c] = _dot_nt(wvt_ref[...], h[toks]).astype(BF16)

    kv_tiles = TOKEN_TILE // KV_TILE
    vts = [functools.partial(proj_vt, c) for c in range(kv_tiles)]
    half = (kv_tiles + 1) // 2
    matmuls = [proj_q] + vts[:half] + [proj_k] + vts[half:] + [proj_xb, proj_gate]
    n_slabs = TOKEN_TILE // LRU_ROWS
    state = hstate_ref[...]
    for slab in range(n_slabs):
        xc, gates = _lru_conv_gates(slab, *conv_refs)
        for piece in matmuls[slab * len(matmuls) // n_slabs:
                             (slab + 1) * len(matmuls) // n_slabs]:
            piece()
        _lru_rows(slab, xc, gates, *row_refs)
        state = _lru_serial(slab, state, *ser_refs)
    hstate_ref[...] = state
    xpad_ref[0:SUBLANES, :] = xpad_ref[TOKEN_TILE:TOKEN_TILE + SUBLANES, :]
    xpad_ref[SUBLANES:SUBLANES + TOKEN_TILE, :] = staged["xb"]
    gate_ref[...] = staged["gate"]


def _ab_in(x2, seq_len, g, w_in, e, conv_w, conv_b, w_gates, ba, bx, lam):
    t = x2.shape[0]
    c = LRU_WIDTH
    assert seq_len % TOKEN_TILE == 0
    n_tiles = t // TOKEN_TILE
    row = lambda i: (jnp.minimum(i, n_tiles - 1), 0)
    prev_row = lambda i: (jnp.maximum(i - 1, 0), 0)
    kv_per_tile = TOKEN_TILE // KV_TILE
    return pl.pallas_call(
        functools.partial(_ab_in_kernel, seq_len // TOKEN_TILE),
        grid=(n_tiles + 1,),
        in_specs=[pl.BlockSpec((TOKEN_TILE, D_MODEL), row),
                  _resident((1, D_MODEL)),
                  _resident_layer(w_in, e),
                  _resident((CONV_WIDTH, c)), _resident((1, c)),
                  _resident((c, 2 * c)), _resident((1, c)), _resident((1, c)),
                  _resident((1, c))],
        out_specs=[pl.BlockSpec((TOKEN_TILE, DIFF_WIDTH), row)] * 2
        + [pl.BlockSpec((kv_per_tile, DIFF_WIDTH, KV_TILE), lambda i: row(i) + (0,))]
        + [pl.BlockSpec((TOKEN_TILE, c), prev_row)],
        out_shape=[jax.ShapeDtypeStruct((t, DIFF_WIDTH), BF16)] * 2
        + [jax.ShapeDtypeStruct((t // KV_TILE, DIFF_WIDTH, KV_TILE), BF16)]
        + [jax.ShapeDtypeStruct((t, c), BF16)],
        scratch_shapes=[pltpu.VMEM((DIFF_WIDTH, D_MODEL), BF16),
                        pltpu.VMEM((TOKEN_TILE + SUBLANES, c), F32),
                        pltpu.VMEM((TOKEN_TILE, c), F32),
                        pltpu.VMEM((SUBLANES, c), F32)]
        + [pltpu.VMEM((TOKEN_TILE // SUBLANES, SUBLANES, c), F32)] * 2,
        compiler_params=pltpu.CompilerParams(
            dimension_semantics=("arbitrary",), vmem_limit_bytes=VMEM_LIMIT_BYTES),
        name="ab_in_lru",
    )(x2, g, w_in, conv_w, conv_b, w_gates, ba, bx, lam)


def _attn_kernel(lam_init, q_ref, k_ref, vt_ref, lq1_ref, lk1_ref, lq2_ref, lk2_ref,
                 sg_ref, o_ref, m_ref, acc_ref, s_ref):
    first_q = pl.program_id(1) * Q_GROUPS
    d = DIFF_HEAD_DIM
    vd = DIFF_V_DIM
    per_tile = 2 * DIFF_HEADS
    n_chains = Q_GROUPS * per_tile

    m_ref[...] = jnp.full(m_ref.shape, NEG_BIG, F32)
    acc_ref[...] = jnp.zeros(acc_ref.shape, F32)
    ones = jnp.ones((BF16_SUBLANES, KV_TILE), BF16)

    lane = lax.broadcasted_iota(jnp.int32, (Q_TILE, vd), 1)
    qz = []
    for c in range(n_chains):
        t, h, mp = c // per_tile, (c % per_tile) // 2, c % 2
        qh = q_ref[t * Q_TILE:(t + 1) * Q_TILE, h * vd:(h + 1) * vd]
        keep = (lane >= d) if mp else (lane < d)
        qz.append(jnp.where(keep, qh, jnp.zeros_like(qh)))

    def head_of(c):
        return (c % per_tile) // 2

    def scores(c, j):
        h = head_of(c)
        start = pl.multiple_of(j * KV_TILE, KV_TILE)
        kb = k_ref[pl.ds(start, KV_TILE), h * vd:(h + 1) * vd]
        return _dot_nt(kb, qz[c])

    def softmax_update(c, s):
        m_prev = m_ref[c]
        m_new = jnp.maximum(m_prev, jnp.max(s, axis=0, keepdims=True))
        alpha = jnp.exp2(m_prev - m_new)
        p = jnp.exp2(s - m_new)
        m_ref[c] = m_new
        return p.astype(BF16), alpha

    def accumulate(c, j, p, alpha):
        h = head_of(c)
        vb = vt_ref[j, h * vd:(h + 1) * vd, :]
        acc_ref[c] = alpha * acc_ref[c] + _dot(jnp.concatenate([vb, ones], axis=0), p)

    def step(j, chains, masked, prefetch):
        pending = None
        for c in chains:
            s = s_ref[c]
            if c in prefetch:
                s_ref[c] = scores(c, j + 1)
            if c in masked:
                s = jnp.where(diagonal, s, NEG_BIG)
            p, alpha = softmax_update(c, s)
            if pending is not None:
                accumulate(*pending)
            pending = (c, j, p, alpha)
        accumulate(*pending)

    everything = range(n_chains)
    for c in everything:
        s_ref[c] = scores(c, 0)

    def body(j, carry):
        step(j, everything, (), everything)
        return carry

    lax.fori_loop(0, first_q, body, 0)
    kc = lax.broadcasted_iota(jnp.int32, (KV_TILE, Q_TILE), 0) // CHUNK
    qc = lax.broadcasted_iota(jnp.int32, (KV_TILE, Q_TILE), 1) // CHUNK
    diagonal = kc <= qc
    for t in range(Q_GROUPS):
        tile = range(t * per_tile, (t + 1) * per_tile)
        later = range((t + 1) * per_tile, n_chains)
        step(first_q + t, range(t * per_tile, n_chains), tile, later)

    lam = (jnp.exp(jnp.sum(lq1_ref[...] * lk1_ref[...], axis=-1, keepdims=True))
           - jnp.exp(jnp.sum(lq2_ref[...] * lk2_ref[...], axis=-1, keepdims=True))
           + lam_init)
    def normalized(c):
        return acc_ref[c, 0:vd, :] * (1.0 / acc_ref[c, vd:vd + 1, :])

    for c in range(0, n_chains, 2):
        t, h = c // per_tile, head_of(c)
        o = normalized(c) - lam * normalized(c + 1)
        o = o * lax.rsqrt(jnp.mean(o * o, axis=0, keepdims=True) + EPS)
        o = o * (sg_ref[...] * (1.0 - lam_init))
        o_ref[t * Q_TILE:(t + 1) * Q_TILE, h * vd:(h + 1) * vd] = o.T.astype(o_ref.dtype)


def _diff_attention(q, k, vt, lq1, lk1, lq2, lk2, subln_col, lam_init):
    b, s, w = q.shape
    q_rows = Q_GROUPS * Q_TILE
    assert Q_TILE == KV_TILE and Q_TILE % CHUNK == 0 and s % q_rows == 0
    vd = DIFF_V_DIM
    n_maps = Q_GROUPS * 2 * DIFF_HEADS
    vec = lambda n: _resident((1, n))
    return pl.pallas_call(
        functools.partial(_attn_kernel, lam_init),
        grid=(b, s // q_rows),
        in_specs=[pl.BlockSpec((None, q_rows, w), lambda bi, i: (bi, i, 0)),
                  pl.BlockSpec((None, s, w), lambda bi, i: (bi, 0, 0)),
                  pl.BlockSpec((s // KV_TILE, w, KV_TILE), lambda bi, i: (bi, 0, 0)),
                  vec(DIFF_HEAD_DIM), vec(DIFF_HEAD_DIM), vec(DIFF_HEAD_DIM),
                  vec(DIFF_HEAD_DIM), _resident((vd, 1))],
        out_specs=pl.BlockSpec((None, q_rows, w), lambda bi, i: (bi, i, 0)),
        out_shape=jax.ShapeDtypeStruct((b, s, w), BF16),
        scratch_shapes=[pltpu.VMEM((n_maps, 1, Q_TILE), F32),
                        pltpu.VMEM((n_maps, vd + BF16_SUBLANES, Q_TILE), F32),
                        pltpu.VMEM((n_maps, KV_TILE, Q_TILE), F32)],
        compiler_params=pltpu.CompilerParams(
            dimension_semantics=("arbitrary",) * 2, vmem_limit_bytes=VMEM_LIMIT_BYTES),
        name="diff_attention",
    )(q, k, vt, lq1, lk1, lq2, lk2, subln_col)


def _lru_conv_gates(slab, cw_ref, cb_ref, wg_ref, xpad_ref):
    pad = SUBLANES
    rows = LRU_ROWS
    xc = cb_ref[...]
    for j in range(CONV_WIDTH):
        off = slab * rows + pad - (CONV_WIDTH - 1) + j
        xc = xc + cw_ref[j:j + 1, :] * xpad_ref[off:off + rows, :]
    return xc, _dot(xc.astype(BF16), wg_ref[...])


def _lru_rows(slab, xc, gates, ba_ref, bx_ref, lam_ref, aloc_ref, bloc_ref):
    r = _sigmoid(gates[:, :LRU_WIDTH] + ba_ref[...])
    ig = _sigmoid(gates[:, LRU_WIDTH:] + bx_ref[...])
    z = -lam_ref[...]
    softplus = jnp.maximum(z, 0.0) + jnp.log1p(jnp.exp(-jnp.abs(z)))
    log_a = -LRU_C * r * softplus
    a = jnp.exp(log_a)
    bt = jnp.sqrt(-jnp.tanh(log_a) * (a * a + 1.0)) * (ig * xc)

    groups = LRU_ROWS // SUBLANES
    a3 = a.reshape(groups, SUBLANES, LRU_WIDTH)
    b3 = bt.reshape(groups, SUBLANES, LRU_WIDTH)
    sub = lax.broadcasted_iota(jnp.int32, a3.shape, 1)
    shift = 1
    while shift < SUBLANES:
        keep = sub >= shift
        a_prev = jnp.where(keep, pltpu.roll(a3, shift, 1), 1.0)
        b_prev = jnp.where(keep, pltpu.roll(b3, shift, 1), 0.0)
        b3 = a3 * b_prev + b3
        a3 = a3 * a_prev
        shift *= 2
    aloc_ref[slab * groups:(slab + 1) * groups] = a3
    bloc_ref[slab * groups:(slab + 1) * groups] = b3


def _lru_serial(slab, h_in, gate_ref, aloc_ref, bloc_ref, yb_ref):
    groups = LRU_ROWS // SUBLANES
    hs = []
    for g in range(slab * groups, (slab + 1) * groups):
        hg = aloc_ref[g] * h_in + bloc_ref[g]
        hs.append(hg)
        h_in = jnp.broadcast_to(hg[SUBLANES - 1:SUBLANES, :], hg.shape)
    rows = slice(slab * LRU_ROWS, (slab + 1) * LRU_ROWS)
    yb = jax.nn.gelu(gate_ref[rows, :]) * jnp.concatenate(hs, axis=0)
    yb_ref[rows, :] = yb.astype(yb_ref.dtype)
    return h_in


def _block_diag(w):
    nb, d, _ = w.shape
    eye = jnp.eye(nb, dtype=w.dtype)
    return (eye[:, None, :, None] * w[:, :, None, :]).reshape(nb * d, nb * d)


def _ffn(x, g, w1_ref, w2_ref):
    h = _rms(x, g).astype(BF16)
    acc = None
    for c in range(D_FF // FF_CHUNK):
        cols = slice(c * FF_CHUNK, (c + 1) * FF_CHUNK)
        t = jnp.maximum(_dot(h, w1_ref[:, cols]), 0.0)
        part = _dot((t * t).astype(BF16), w2_ref[cols, :])
        acc = part if acc is None else acc + part
    return x + acc


def _finish(x, gfin_ref, o_ref):
    if gfin_ref is not None:
        x = _rms(x, gfin_ref[...])
    o_ref[...] = x


def _ab_out_ffn_kernel(final, x_ref, ya_ref, yb_ref, wo_ref, gf_ref, w1_ref, w2_ref,
                       *rest):
    gfin_ref, o_ref = rest if final else (None, rest[0])
    y = jnp.concatenate([ya_ref[...], yb_ref[...]], axis=1)
    x = x_ref[...] + _dot(y, wo_ref[...])
    _finish(_ffn(x, gf_ref[...], w1_ref, w2_ref), gfin_ref, o_ref)


def _ab_out_ffn(x2, ya, yb, w_out, e, g_ffn, w1, w2, layer, g_final):
    t = x2.shape[0]
    row = lambda i: (i, 0)
    final = g_final is not None
    in_specs = [pl.BlockSpec((MLP_TILE, D_MODEL), row),
                pl.BlockSpec((MLP_TILE, DIFF_WIDTH), row),
                pl.BlockSpec((MLP_TILE, LRU_WIDTH), row),
                _resident_layer(w_out, e), _resident((1, D_MODEL)),
                _resident_layer(w1, layer), _resident_layer(w2, layer)]
    args = [x2, ya, yb, w_out, g_ffn, w1, w2]
    if final:
        in_specs.append(_resident((1, D_MODEL)))
        args.append(g_final)
    return pl.pallas_call(
        functools.partial(_ab_out_ffn_kernel, final),
        grid=(t // MLP_TILE,),
        in_specs=in_specs,
        out_specs=pl.BlockSpec((MLP_TILE, D_MODEL), row),
        out_shape=jax.ShapeDtypeStruct((t, D_MODEL), F32),
        compiler_params=pltpu.CompilerParams(
            dimension_semantics=("arbitrary",), vmem_limit_bytes=VMEM_LIMIT_BYTES),
        name="ab_out_ffn",
    )(*args)


def _sgu_ffn_kernel(final, x_ref, gm_ref, win_ref, lng_ref, lnb_ref, ws_ref, bs_ref,
                    wout_ref, gf_ref, w1_ref, w2_ref, *rest):
    if final:
        gfin_ref, o_ref, gated_ref = rest
    else:
        gfin_ref, (o_ref, gated_ref) = None, rest
    x = x_ref[...]
    h = _rms(x, gm_ref[...]).astype(BF16)
    u_cols = slice(0, SGU_WIDTH)
    v_cols = slice(SGU_WIDTH, 2 * SGU_WIDTH)
    v = jax.nn.gelu(_dot(h, win_ref[:, v_cols]))
    mu = jnp.mean(v, axis=-1, keepdims=True)
    vc = v - mu
    var = jnp.mean(vc * vc, axis=-1, keepdims=True)
    vn = (vc * lax.rsqrt(var + EPS) * lng_ref[...] + lnb_ref[...]).astype(BF16)
    u = jax.nn.gelu(_dot(h, win_ref[:, u_cols]))

    pc = lax.broadcasted_iota(jnp.int32, (SGU_BLOCK, SGU_BLOCK), 0) // CHUNK
    qc = lax.broadcasted_iota(jnp.int32, (SGU_BLOCK, SGU_BLOCK), 1) // CHUNK
    mask = qc <= pc
    for g in range(SGU_GROUPS):
        cols = slice(g * SGU_GROUP_DIM, (g + 1) * SGU_GROUP_DIM)
        w = jnp.where(mask, ws_ref[g], 0.0).astype(BF16)
        bias = bs_ref[:, g:g + 1]
        for n in range(SGU_TILE // SGU_BLOCK):
            rows = slice(n * SGU_BLOCK, (n + 1) * SGU_BLOCK)
            mixed = _dot(w, vn[rows, cols]) + bias
            gated_ref[rows, cols] = (u[rows, cols] * mixed).astype(BF16)

    x = x + _dot(gated_ref[...], wout_ref[...])
    _finish(_ffn(x, gf_ref[...], w1_ref, w2_ref), gfin_ref, o_ref)


def _sgu_ffn(x2, g_mix, w_in, ln_g, ln_b, w_s, b_s_t, w_out, o, g_ffn, w1, w2, layer,
             g_final):
    t = x2.shape[0]
    row = lambda i: (i, 0)
    final = g_final is not None
    in_specs = [pl.BlockSpec((SGU_TILE, D_MODEL), row),
                _resident((1, D_MODEL)), _resident_layer(w_in, o),
                _resident((1, SGU_WIDTH)), _resident((1, SGU_WIDTH)),
                _resident_layer(w_s, o), _resident(b_s_t.shape),
                _resident_layer(w_out, o), _resident((1, D_MODEL)),
                _resident_layer(w1, layer), _resident_layer(w2, layer)]
    args = [x2, g_mix, w_in, ln_g, ln_b, w_s, b_s_t, w_out, g_ffn, w1, w2]
    if final:
        in_specs.append(_resident((1, D_MODEL)))
        args.append(g_final)
    return pl.pallas_call(
        functools.partial(_sgu_ffn_kernel, final),
        grid=(t // SGU_TILE,),
        in_specs=in_specs,
        out_specs=pl.BlockSpec((SGU_TILE, D_MODEL), row),
        out_shape=jax.ShapeDtypeStruct((t, D_MODEL), F32),
        scratch_shapes=[pltpu.VMEM((SGU_TILE, SGU_WIDTH), BF16)],
        compiler_params=pltpu.CompilerParams(
            dimension_semantics=("arbitrary",), vmem_limit_bytes=VMEM_LIMIT_BYTES),
        name="sgu_ffn",
    )(*args)


def kernel(x, norm_mix, norm_ffn, norm_final, ab_w_in, diff_lq1, diff_lk1, diff_lq2,
           diff_lk2, diff_subln, lru_conv_w, lru_conv_b, lru_wa, lru_ba, lru_wx, lru_bx,
           lru_lambda, ab_w_out, c_w_in, c_ln_g, c_ln_b, c_w_s, c_b_s, c_w_out, ffn_w1,
           ffn_w2):
    b, s, d = x.shape
    t = b * s
    assert t % TOKEN_TILE == 0 and TOKEN_TILE % SGU_BLOCK == 0
    x2 = x.reshape(t, d)
    vec = lambda a: a.reshape(1, -1)
    w1, w2 = ffn_w1.astype(BF16), ffn_w2.astype(BF16)
    w_in_ab, w_out_ab = ab_w_in.astype(BF16), ab_w_out.astype(BF16)
    w_in_c, w_out_c = c_w_in.astype(BF16), c_w_out.astype(BF16)
    for layer in range(DEPTH):
        g_final = vec(norm_final) if layer == DEPTH - 1 else None
        g_ffn = vec(norm_ffn[layer])
        if layer % 2 == 0:
            e = layer // 2
            lam_init = 0.8 - 0.6 * math.exp(-0.3 * layer)
            w_gates = jnp.concatenate(
                [_block_diag(lru_wa[e]), _block_diag(lru_wx[e])], axis=1).astype(BF16)
            q, k, vt, yb = _ab_in(
                x2, s, vec(norm_mix[layer]), w_in_ab, e, lru_conv_w[e], vec(lru_conv_b[e]),
                w_gates, vec(lru_ba[e]), vec(lru_bx[e]), vec(lru_lambda[e]))
            seq = lambda a: a.reshape(b, s, a.shape[-1])
            ya = _diff_attention(seq(q), seq(k), vt, vec(diff_lq1[e]), vec(diff_lk1[e]),
                                 vec(diff_lq2[e]), vec(diff_lk2[e]),
                                 diff_subln[e].reshape(-1, 1), lam_init)
            x2 = _ab_out_ffn(x2, ya.reshape(t, -1), yb, w_out_ab, e, g_ffn, w1, w2, layer,
                             g_final)
        else:
            o = layer // 2
            x2 = _sgu_ffn(x2, vec(norm_mix[layer]), w_in_c, vec(c_ln_g[o]), vec(c_ln_b[o]),
                          c_w_s, c_b_s[o].T, w_out_c, o, g_ffn, w1, w2, layer, g_final)
    return x2.reshape(b, s, d)
```

```python
import functools
import math

import jax
import jax.numpy as jnp
from jax import lax
from jax.experimental import pallas as pl
from jax.experimental.pallas import tpu as pltpu

D_MODEL = 1024
DEPTH = 4
CHUNK = 64
DIFF_HEADS = 4
DIFF_HEAD_DIM = 64
DIFF_V_DIM = 2 * DIFF_HEAD_DIM
DIFF_WIDTH = DIFF_HEADS * DIFF_V_DIM
LRU_WIDTH = 512
LRU_BLOCKS = 8
LRU_BLOCK_DIM = LRU_WIDTH // LRU_BLOCKS
LRU_C = 8.0
CONV_WIDTH = 4
SGU_WIDTH = D_MODEL
SGU_GROUPS = 8
SGU_GROUP_DIM = SGU_WIDTH // SGU_GROUPS
SGU_BLOCK = 128
D_FF = 4 * D_MODEL
EPS = 1e-6

F32 = jnp.float32
BF16 = jnp.bfloat16

VMEM_LIMIT_BYTES = 56 * 1024 * 1024
SUBLANES = 8
BF16_SUBLANES = 16
LOG2_E = math.log2(math.e)

TOKEN_TILE = 512
MLP_TILE = 1024
SGU_TILE = 1024
FF_CHUNK = 1024
Q_TILE = 256
Q_GROUPS = 4
KV_TILE = 256
LRU_ROWS = 128
NEG_BIG = -1e30


def _resident(shape):
    return pl.BlockSpec(shape, lambda *_: (0,) * len(shape),
                        pipeline_mode=pl.Buffered(1))


def _resident_layer(stacked, layer):
    shape = stacked.shape[1:]
    return pl.BlockSpec((None,) + shape, lambda *_: (layer,) + (0,) * len(shape),
                        pipeline_mode=pl.Buffered(1))


def _rms(x, g):
    return x * lax.rsqrt(jnp.mean(x * x, axis=-1, keepdims=True) + EPS) * g


def _sigmoid(x):
    return 0.5 * jnp.tanh(0.5 * x) + 0.5


def _dot(a, b):
    return jnp.dot(a, b, preferred_element_type=F32)


def _dot_nt(a, b):
    return lax.dot_general(a, b, (((1,), (1,)), ((), ())), preferred_element_type=F32)


def _ab_in_kernel(tiles_per_seq, x_ref, g_ref, w_ref, cw_ref, cb_ref, wg_ref,
                  ba_ref, bx_ref, lam_ref, q_ref, k_ref, vt_ref, yb_ref, wvt_ref,
                  *lru_scratch):
    i = pl.program_id(0)
    xpad_ref, gate_ref, hstate_ref, aloc_ref, bloc_ref = lru_scratch
    w = DIFF_WIDTH

    @pl.when(i == 0)
    def _():
        xpad_ref[...] = jnp.zeros(xpad_ref.shape, F32)
        gate_ref[...] = jnp.zeros(gate_ref.shape, F32)
        hstate_ref[...] = jnp.zeros(hstate_ref.shape, F32)
        wvt_ref[...] = w_ref[:, 2 * w:3 * w].T

    @pl.when(lax.rem(i - 1, tiles_per_seq) == 0)
    def _():
        xpad_ref[0:SUBLANES, :] = jnp.zeros((SUBLANES, LRU_WIDTH), F32)
        hstate_ref[...] = jnp.zeros(hstate_ref.shape, F32)

    h = _rms(x_ref[...], g_ref[...]).astype(BF16)
    q_scale = DIFF_HEAD_DIM ** -0.5 * LOG2_E
    conv_refs = (cw_ref, cb_ref, wg_ref, xpad_ref)
    row_refs = (ba_ref, bx_ref, lam_ref, aloc_ref, bloc_ref)
    ser_refs = (gate_ref, aloc_ref, bloc_ref, yb_ref)
    staged = {}

    def proj_q():
        q_ref[...] = (_dot(h, w_ref[:, 0:w]) * q_scale).astype(BF16)

    def proj_k():
        k_ref[...] = _dot(h, w_ref[:, w:2 * w]).astype(BF16)

    def proj_xb():
        staged["xb"] = _dot(h, w_ref[:, 3 * w:3 * w + LRU_WIDTH])

    def proj_gate():
        staged["gate"] = _dot(h, w_ref[:, 3 * w + LRU_WIDTH:])

    def proj_vt(c):
        toks = slice(c * KV_TILE, (c + 1) * KV_TILE)
        vt_ref[c] = _dot_nt(wvt_ref[...], h[toks]).astype(BF16)

    kv_tiles = TOKEN_TILE // KV_TILE
    vts = [functools.partial(proj_vt, c) for c in range(kv_tiles)]
    half = (kv_tiles + 1) // 2
    matmuls = [proj_q] + vts[:half] + [proj_k] + vts[half:] + [proj_xb, proj_gate]
    n_slabs = TOKEN_TILE // LRU_ROWS
    state = hstate_ref[...]
    for slab in range(n_slabs):
        xc, gates = _lru_conv_gates(slab, *conv_refs)
        for piece in matmuls[slab * len(matmuls) // n_slabs:
                             (slab + 1) * len(matmuls) // n_slabs]:
            piece()
        _lru_rows(slab, xc, gates, *row_refs)
        state = _lru_serial(slab, state, *ser_refs)
    hstate_ref[...] = state
    xpad_ref[0:SUBLANES, :] = xpad_ref[TOKEN_TILE:TOKEN_TILE + SUBLANES, :]
    xpad_ref[SUBLANES:SUBLANES + TOKEN_TILE, :] = staged["xb"]
    gate_ref[...] = staged["gate"]


def _ab_in(x2, seq_len, g, w_in, e, conv_w, conv_b, w_gates, ba, bx, lam):
    t = x2.shape[0]
    c = LRU_WIDTH
    assert seq_len % TOKEN_TILE == 0
    n_tiles = t // TOKEN_TILE
    row = lambda i: (jnp.minimum(i, n_tiles - 1), 0)
    prev_row = lambda i: (jnp.maximum(i - 1, 0), 0)
    kv_per_tile = TOKEN_TILE // KV_TILE
    return pl.pallas_call(
        functools.partial(_ab_in_kernel, seq_len // TOKEN_TILE),
        grid=(n_tiles + 1,),
        in_specs=[pl.BlockSpec((TOKEN_TILE, D_MODEL), row),
                  _resident((1, D_MODEL)),
                  _resident_layer(w_in, e),
                  _resident((CONV_WIDTH, c)), _resident((1, c)),
                  _resident((c, 2 * c)), _resident((1, c)), _resident((1, c)),
                  _resident((1, c))],
        out_specs=[pl.BlockSpec((TOKEN_TILE, DIFF_WIDTH), row)] * 2
        + [pl.BlockSpec((kv_per_tile, DIFF_WIDTH, KV_TILE), lambda i: row(i) + (0,))]
        + [pl.BlockSpec((TOKEN_TILE, c), prev_row)],
        out_shape=[jax.ShapeDtypeStruct((t, DIFF_WIDTH), BF16)] * 2
        + [jax.ShapeDtypeStruct((t // KV_TILE, DIFF_WIDTH, KV_TILE), BF16)]
        + [jax.ShapeDtypeStruct((t, c), BF16)],
        scratch_shapes=[pltpu.VMEM((DIFF_WIDTH, D_MODEL), BF16),
                        pltpu.VMEM((TOKEN_TILE + SUBLANES, c), F32),
                        pltpu.VMEM((TOKEN_TILE, c), F32),
                        pltpu.VMEM((SUBLANES, c), F32)]
        + [pltpu.VMEM((TOKEN_TILE // SUBLANES, SUBLANES, c), F32)] * 2,
        compiler_params=pltpu.CompilerParams(
            dimension_semantics=("arbitrary",), vmem_limit_bytes=VMEM_LIMIT_BYTES),
        name="ab_in_lru",
    )(x2, g, w_in, conv_w, conv_b, w_gates, ba, bx, lam)


def _attn_kernel(lam_init, q_ref, k_ref, vt_ref, lq1_ref, lk1_ref, lq2_ref, lk2_ref,
                 sg_ref, o_ref, m_ref, acc_ref, s_ref):
    first_q = pl.program_id(1) * Q_GROUPS
    d = DIFF_HEAD_DIM
    vd = DIFF_V_DIM
    per_tile = 2 * DIFF_HEADS
    n_chains = Q_GROUPS * per_tile

    m_ref[...] = jnp.full(m_ref.shape, NEG_BIG, F32)
    acc_ref[...] = jnp.zeros(acc_ref.shape, F32)
    ones = jnp.ones((BF16_SUBLANES, KV_TILE), BF16)

    lane = lax.broadcasted_iota(jnp.int32, (Q_TILE, vd), 1)
    qz = []
    for c in range(n_chains):
        t, h, mp = c // per_tile, (c % per_tile) // 2, c % 2
        qh = q_ref[t * Q_TILE:(t + 1) * Q_TILE, h * vd:(h + 1) * vd]
        keep = (lane >= d) if mp else (lane < d)
        qz.append(jnp.where(keep, qh, jnp.zeros_like(qh)))

    def head_of(c):
        return (c % per_tile) // 2

    def scores(c, j):
        h = head_of(c)
        start = pl.multiple_of(j * KV_TILE, KV_TILE)
        kb = k_ref[pl.ds(start, KV_TILE), h * vd:(h + 1) * vd]
        return _dot_nt(kb, qz[c])

    def softmax_update(c, s):
        m_prev = m_ref[c]
        m_new = jnp.maximum(m_prev, jnp.max(s, axis=0, keepdims=True))
        alpha = jnp.exp2(m_prev - m_new)
        p = jnp.exp2(s - m_new)
        m_ref[c] = m_new
        return p.astype(BF16), alpha

    def accumulate(c, j, p, alpha):
        h = head_of(c)
        vb = vt_ref[j, h * vd:(h + 1) * vd, :]
        acc_ref[c] = alpha * acc_ref[c] + _dot(jnp.concatenate([vb, ones], axis=0), p)

    def step(j, chains, masked, prefetch):
        pending = None
        for c in chains:
            s = s_ref[c]
            if c in prefetch:
                s_ref[c] = scores(c, j + 1)
            if c in masked:
                s = jnp.where(diagonal, s, NEG_BIG)
            p, alpha = softmax_update(c, s)
            if pending is not None:
                accumulate(*pending)
            pending = (c, j, p, alpha)
        accumulate(*pending)

    everything = range(n_chains)
    for c in everything:
        s_ref[c] = scores(c, 0)

    def body(j, carry):
        step(j, everything, (), everything)
        return carry

    lax.fori_loop(0, first_q, body, 0)
    kc = lax.broadcasted_iota(jnp.int32, (KV_TILE, Q_TILE), 0) // CHUNK
    qc = lax.broadcasted_iota(jnp.int32, (KV_TILE, Q_TILE), 1) // CHUNK
    diagonal = kc <= qc
    for t in range(Q_GROUPS):
        tile = range(t * per_tile, (t + 1) * per_tile)
        later = range((t + 1) * per_tile, n_chains)
        step(first_q + t, range(t * per_tile, n_chains), tile, later)

    lam = (jnp.exp(jnp.sum(lq1_ref[...] * lk1_ref[...], axis=-1, keepdims=True))
           - jnp.exp(jnp.sum(lq2_ref[...] * lk2_ref[...], axis=-1, keepdims=True))
           + lam_init)
    def normalized(c):
        return acc_ref[c, 0:vd, :] * (1.0 / acc_ref[c, vd:vd + 1, :])

    for c in range(0, n_chains, 2):
        t, h = c // per_tile, head_of(c)
        o = normalized(c) - lam * normalized(c + 1)
        o = o * lax.rsqrt(jnp.mean(o * o, axis=0, keepdims=True) + EPS)
        o = o * (sg_ref[...] * (1.0 - lam_init))
        o_ref[t * Q_TILE:(t + 1) * Q_TILE, h * vd:(h + 1) * vd] = o.T.astype(o_ref.dtype)


def _diff_attention(q, k, vt, lq1, lk1, lq2, lk2, subln_col, lam_init):
    b, s, w = q.shape
    q_rows = Q_GROUPS * Q_TILE
    assert Q_TILE == KV_TILE and Q_TILE % CHUNK == 0 and s % q_rows == 0
    vd = DIFF_V_DIM
    n_maps = Q_GROUPS * 2 * DIFF_HEADS
    vec = lambda n: _resident((1, n))
    return pl.pallas_call(
        functools.partial(_attn_kernel, lam_init),
        grid=(b, s // q_rows),
        in_specs=[pl.BlockSpec((None, q_rows, w), lambda bi, i: (bi, i, 0)),
                  pl.BlockSpec((None, s, w), lambda bi, i: (bi, 0, 0)),
                  pl.BlockSpec((s // KV_TILE, w, KV_TILE), lambda bi, i: (bi, 0, 0)),
                  vec(DIFF_HEAD_DIM), vec(DIFF_HEAD_DIM), vec(DIFF_HEAD_DIM),
                  vec(DIFF_HEAD_DIM), _resident((vd, 1))],
        out_specs=pl.BlockSpec((None, q_rows, w), lambda bi, i: (bi, i, 0)),
        out_shape=jax.ShapeDtypeStruct((b, s, w), BF16),
        scratch_shapes=[pltpu.VMEM((n_maps, 1, Q_TILE), F32),
                        pltpu.VMEM((n_maps, vd + BF16_SUBLANES, Q_TILE), F32),
                        pltpu.VMEM((n_maps, KV_TILE, Q_TILE), F32)],
        compiler_params=pltpu.CompilerParams(
            dimension_semantics=("arbitrary",) * 2, vmem_limit_bytes=VMEM_LIMIT_BYTES),
        name="diff_attention",
    )(q, k, vt, lq1, lk1, lq2, lk2, subln_col)


def _lru_conv_gates(slab, cw_ref, cb_ref, wg_ref, xpad_ref):
    rows = LRU_ROWS
    groups = rows // SUBLANES
    xe = xpad_ref[slab * rows:slab * rows + rows + SUBLANES, :]
    xe = xe.reshape(groups + 1, SUBLANES, LRU_WIDTH)
    sub = lax.broadcasted_iota(jnp.int32, (groups, SUBLANES, LRU_WIDTH), 1)
    xc = cb_ref[...] + cw_ref[CONV_WIDTH - 1:CONV_WIDTH, :] * xe[1:]
    for back in range(1, CONV_WIDTH):
        rolled = pltpu.roll(xe, back, 1)
        shifted = jnp.where(sub >= back, rolled[1:], rolled[:-1])
        tap = CONV_WIDTH - 1 - back
        xc = xc + cw_ref[tap:tap + 1, :] * shifted
    xc = xc.reshape(rows, LRU_WIDTH)
    return xc, _dot(xc.astype(BF16), wg_ref[...])


def _lru_rows(slab, xc, gates, ba_ref, bx_ref, lam_ref, aloc_ref, bloc_ref):
    r = _sigmoid(gates[:, :LRU_WIDTH] + ba_ref[...])
    ig = _sigmoid(gates[:, LRU_WIDTH:] + bx_ref[...])
    z = -lam_ref[...]
    softplus = jnp.maximum(z, 0.0) + jnp.log1p(jnp.exp(-jnp.abs(z)))
    log_a = -LRU_C * r * softplus
    a = jnp.exp(log_a)
    bt = jnp.sqrt(-jnp.tanh(log_a) * (a * a + 1.0)) * (ig * xc)

    groups = LRU_ROWS // SUBLANES
    a3 = a.reshape(groups, SUBLANES, LRU_WIDTH)
    b3 = bt.reshape(groups, SUBLANES, LRU_WIDTH)
    sub = lax.broadcasted_iota(jnp.int32, a3.shape, 1)
    shift = 1
    while shift < SUBLANES:
        keep = sub >= shift
        a_prev = jnp.where(keep, pltpu.roll(a3, shift, 1), 1.0)
        b_prev = jnp.where(keep, pltpu.roll(b3, shift, 1), 0.0)
        b3 = a3 * b_prev + b3
        a3 = a3 * a_prev
        shift *= 2
    aloc_ref[slab * groups:(slab + 1) * groups] = a3
    bloc_ref[slab * groups:(slab + 1) * groups] = b3


def _lru_serial(slab, h_in, gate_ref, aloc_ref, bloc_ref, yb_ref):
    groups = LRU_ROWS // SUBLANES
    hs = []
    for g in range(slab * groups, (slab + 1) * groups):
        hg = aloc_ref[g] * h_in + bloc_ref[g]
        hs.append(hg)
        h_in = jnp.broadcast_to(hg[SUBLANES - 1:SUBLANES, :], hg.shape)
    rows = slice(slab * LRU_ROWS, (slab + 1) * LRU_ROWS)
    yb = jax.nn.gelu(gate_ref[rows, :]) * jnp.concatenate(hs, axis=0)
    yb_ref[rows, :] = yb.astype(yb_ref.dtype)
    return h_in


def _block_diag(w):
    nb, d, _ = w.shape
    eye = jnp.eye(nb, dtype=w.dtype)
    return (eye[:, None, :, None] * w[:, :, None, :]).reshape(nb * d, nb * d)


def _ffn(x, g, w1_ref, w2_ref):
    h = _rms(x, g).astype(BF16)
    acc = None
    for c in range(D_FF // FF_CHUNK):
        cols = slice(c * FF_CHUNK, (c + 1) * FF_CHUNK)
        t = jnp.maximum(_dot(h, w1_ref[:, cols]), 0.0)
        part = _dot((t * t).astype(BF16), w2_ref[cols, :])
        acc = part if acc is None else acc + part
    return x + acc


def _finish(x, gfin_ref, o_ref):
    if gfin_ref is not None:
        x = _rms(x, gfin_ref[...])
    o_ref[...] = x


def _ab_out_ffn_kernel(final, x_ref, ya_ref, yb_ref, wo_ref, gf_ref, w1_ref, w2_ref,
                       *rest):
    gfin_ref, o_ref = rest if final else (None, rest[0])
    y = jnp.concatenate([ya_ref[...], yb_ref[...]], axis=1)
    x = x_ref[...] + _dot(y, wo_ref[...])
    _finish(_ffn(x, gf_ref[...], w1_ref, w2_ref), gfin_ref, o_ref)


def _ab_out_ffn(x2, ya, yb, w_out, e, g_ffn, w1, w2, layer, g_final):
    t = x2.shape[0]
    row = lambda i: (i, 0)
    final = g_final is not None
    in_specs = [pl.BlockSpec((MLP_TILE, D_MODEL), row),
                pl.BlockSpec((MLP_TILE, DIFF_WIDTH), row),
                pl.BlockSpec((MLP_TILE, LRU_WIDTH), row),
                _resident_layer(w_out, e), _resident((1, D_MODEL)),
                _resident_layer(w1, layer), _resident_layer(w2, layer)]
    args = [x2, ya, yb, w_out, g_ffn, w1, w2]
    if final:
        in_specs.append(_resident((1, D_MODEL)))
        args.append(g_final)
    return pl.pallas_call(
        functools.partial(_ab_out_ffn_kernel, final),
        grid=(t // MLP_TILE,),
        in_specs=in_specs,
        out_specs=pl.BlockSpec((MLP_TILE, D_MODEL), row),
        out_shape=jax.ShapeDtypeStruct((t, D_MODEL), F32),
        compiler_params=pltpu.CompilerParams(
            dimension_semantics=("arbitrary",), vmem_limit_bytes=VMEM_LIMIT_BYTES),
        name="ab_out_ffn",
    )(*args)


def _sgu_ffn_kernel(final, x_ref, gm_ref, win_ref, lng_ref, lnb_ref, ws_ref, bs_ref,
                    wout_ref, gf_ref, w1_ref, w2_ref, *rest):
    if final:
        gfin_ref, o_ref, gated_ref = rest
    else:
        gfin_ref, (o_ref, gated_ref) = None, rest
    x = x_ref[...]
    h = _rms(x, gm_ref[...]).astype(BF16)
    u_cols = slice(0, SGU_WIDTH)
    v_cols = slice(SGU_WIDTH, 2 * SGU_WIDTH)
    v = jax.nn.gelu(_dot(h, win_ref[:, v_cols]))
    mu = jnp.mean(v, axis=-1, keepdims=True)
    vc = v - mu
    var = jnp.mean(vc * vc, axis=-1, keepdims=True)
    vn = (vc * lax.rsqrt(var + EPS) * lng_ref[...] + lnb_ref[...]).astype(BF16)
    u = jax.nn.gelu(_dot(h, win_ref[:, u_cols]))

    pc = lax.broadcasted_iota(jnp.int32, (SGU_BLOCK, SGU_BLOCK), 0) // CHUNK
    qc = lax.broadcasted_iota(jnp.int32, (SGU_BLOCK, SGU_BLOCK), 1) // CHUNK
    mask = qc <= pc
    for g in range(SGU_GROUPS):
        cols = slice(g * SGU_GROUP_DIM, (g + 1) * SGU_GROUP_DIM)
        w = jnp.where(mask, ws_ref[g], 0.0).astype(BF16)
        bias = bs_ref[:, g:g + 1]
        for n in range(SGU_TILE // SGU_BLOCK):
            rows = slice(n * SGU_BLOCK, (n + 1) * SGU_BLOCK)
            mixed = _dot(w, vn[rows, cols]) + bias
            gated_ref[rows, cols] = (u[rows, cols] * mixed).astype(BF16)

    x = x + _dot(gated_ref[...], wout_ref[...])
    _finish(_ffn(x, gf_ref[...], w1_ref, w2_ref), gfin_ref, o_ref)


def _sgu_ffn(x2, g_mix, w_in, ln_g, ln_b, w_s, b_s_t, w_out, o, g_ffn, w1, w2, layer,
             g_final):
    t = x2.shape[0]
    row = lambda i: (i, 0)
    final = g_final is not None
    in_specs = [pl.BlockSpec((SGU_TILE, D_MODEL), row),
                _resident((1, D_MODEL)), _resident_layer(w_in, o),
                _resident((1, SGU_WIDTH)), _resident((1, SGU_WIDTH)),
                _resident_layer(w_s, o), _resident(b_s_t.shape),
                _resident_layer(w_out, o), _resident((1, D_MODEL)),
                _resident_layer(w1, layer), _resident_layer(w2, layer)]
    args = [x2, g_mix, w_in, ln_g, ln_b, w_s, b_s_t, w_out, g_ffn, w1, w2]
    if final:
        in_specs.append(_resident((1, D_MODEL)))
        args.append(g_final)
    return pl.pallas_call(
        functools.partial(_sgu_ffn_kernel, final),
        grid=(t // SGU_TILE,),
        in_specs=in_specs,
        out_specs=pl.BlockSpec((SGU_TILE, D_MODEL), row),
        out_shape=jax.ShapeDtypeStruct((t, D_MODEL), F32),
        scratch_shapes=[pltpu.VMEM((SGU_TILE, SGU_WIDTH), BF16)],
        compiler_params=pltpu.CompilerParams(
            dimension_semantics=("arbitrary",), vmem_limit_bytes=VMEM_LIMIT_BYTES),
        name="sgu_ffn",
    )(*args)


def kernel(x, norm_mix, norm_ffn, norm_final, ab_w_in, diff_lq1, diff_lk1, diff_lq2,
           diff_lk2, diff_subln, lru_conv_w, lru_conv_b, lru_wa, lru_ba, lru_wx, lru_bx,
           lru_lambda, ab_w_out, c_w_in, c_ln_g, c_ln_b, c_w_s, c_b_s, c_w_out, ffn_w1,
           ffn_w2):
    b, s, d = x.shape
    t = b * s
    assert t % TOKEN_TILE == 0 and TOKEN_TILE % SGU_BLOCK == 0
    x2 = x.reshape(t, d)
    vec = lambda a: a.reshape(1, -1)
    w1, w2 = ffn_w1.astype(BF16), ffn_w2.astype(BF16)
    w_in_ab, w_out_ab = ab_w_in.astype(BF16), ab_w_out.astype(BF16)
    w_in_c, w_out_c = c_w_in.astype(BF16), c_w_out.astype(BF16)
    for layer in range(DEPTH):
        g_final = vec(norm_final) if layer == DEPTH - 1 else None
        g_ffn = vec(norm_ffn[layer])
        if layer % 2 == 0:
            e = layer // 2
            lam_init = 0.8 - 0.6 * math.exp(-0.3 * layer)
            w_gates = jnp.concatenate(
                [_block_diag(lru_wa[e]), _block_diag(lru_wx[e])], axis=1).astype(BF16)
            q, k, vt, yb = _ab_in(
                x2, s, vec(norm_mix[layer]), w_in_ab, e, lru_conv_w[e], vec(lru_conv_b[e]),
                w_gates, vec(lru_ba[e]), vec(lru_bx[e]), vec(lru_lambda[e]))
            seq = lambda a: a.reshape(b, s, a.shape[-1])
            ya = _diff_attention(seq(q), seq(k), vt, vec(diff_lq1[e]), vec(diff_lk1[e]),
                                 vec(diff_lq2[e]), vec(diff_lk2[e]),
                                 diff_subln[e].reshape(-1, 1), lam_init)
            x2 = _ab_out_ffn(x2, ya.reshape(t, -1), yb, w_out_ab, e, g_ffn, w1, w2, layer,
                             g_final)
        else:
            o = layer // 2
            x2 = _sgu_ffn(x2, vec(norm_mix[layer]), w_in_c, vec(c_ln_g[o]), vec(c_ln_b[o]),
                          c_w_s, c_b_s[o].T, w_out_c, o, g_ffn, w1, w2, layer, g_final)
    return x2.reshape(b, s, d)
```

```python
import functools
import math

import jax
import jax.numpy as jnp
from jax import lax
from jax.experimental import pallas as pl
from jax.experimental.pallas import tpu as pltpu

D_MODEL = 1024
DEPTH = 4
CHUNK = 64
DIFF_HEADS = 4
DIFF_HEAD_DIM = 64
DIFF_V_DIM = 2 * DIFF_HEAD_DIM
DIFF_WIDTH = DIFF_HEADS * DIFF_V_DIM
LRU_WIDTH = 512
LRU_BLOCKS = 8
LRU_BLOCK_DIM = LRU_WIDTH // LRU_BLOCKS
LRU_C = 8.0
CONV_WIDTH = 4
SGU_WIDTH = D_MODEL
SGU_GROUPS = 8
SGU_GROUP_DIM = SGU_WIDTH // SGU_GROUPS
SGU_BLOCK = 128
D_FF = 4 * D_MODEL
EPS = 1e-6

F32 = jnp.float32
BF16 = jnp.bfloat16

VMEM_LIMIT_BYTES = 56 * 1024 * 1024
SUBLANES = 8
BF16_SUBLANES = 16
LOG2_E = math.log2(math.e)

TOKEN_TILE = 512
MLP_TILE = 1024
SGU_TILE = 1024
FF_CHUNK = 1024
Q_TILE = 256
Q_GROUPS = 4
KV_TILE = 256
LRU_ROWS = 128
NEG_BIG = -1e30


def _resident(shape):
    return pl.BlockSpec(shape, lambda *_: (0,) * len(shape),
                        pipeline_mode=pl.Buffered(1))


def _resident_layer(stacked, layer):
    shape = stacked.shape[1:]
    return pl.BlockSpec((None,) + shape, lambda *_: (layer,) + (0,) * len(shape),
                        pipeline_mode=pl.Buffered(1))


def _rms(x, g):
    return x * lax.rsqrt(jnp.mean(x * x, axis=-1, keepdims=True) + EPS) * g


def _dot(a, b):
    return jnp.dot(a, b, preferred_element_type=F32)


def _dot_nt(a, b):
    return lax.dot_general(a, b, (((1,), (1,)), ((), ())), preferred_element_type=F32)


def _ab_in_kernel(tiles_per_seq, x_ref, g_ref, w_ref, cw_ref, cb_ref, wg_ref,
                  ba_ref, bx_ref, lam_ref, q_ref, k_ref, vt_ref, yb_ref, wvt_ref,
                  *lru_scratch):
    i = pl.program_id(0)
    xpad_ref, gate_ref, hstate_ref, aloc_ref, bloc_ref = lru_scratch
    w = DIFF_WIDTH

    @pl.when(i == 0)
    def _():
        xpad_ref[...] = jnp.zeros(xpad_ref.shape, F32)
        gate_ref[...] = jnp.zeros(gate_ref.shape, F32)
        hstate_ref[...] = jnp.zeros(hstate_ref.shape, F32)
        wvt_ref[...] = w_ref[:, 2 * w:3 * w].T

    @pl.when(lax.rem(i - 1, tiles_per_seq) == 0)
    def _():
        xpad_ref[0:SUBLANES, :] = jnp.zeros((SUBLANES, LRU_WIDTH), F32)
        hstate_ref[...] = jnp.zeros(hstate_ref.shape, F32)

    h = _rms(x_ref[...], g_ref[...]).astype(BF16)
    q_scale = DIFF_HEAD_DIM ** -0.5 * LOG2_E
    conv_refs = (cw_ref, cb_ref, wg_ref, xpad_ref)
    row_refs = (ba_ref, bx_ref, lam_ref, aloc_ref, bloc_ref)
    ser_refs = (gate_ref, aloc_ref, bloc_ref, yb_ref)
    staged = {}

    def proj_q():
        q_ref[...] = (_dot(h, w_ref[:, 0:w]) * q_scale).astype(BF16)

    def proj_k():
        k_ref[...] = _dot(h, w_ref[:, w:2 * w]).astype(BF16)

    def proj_xb():
        staged["xb"] = _dot(h, w_ref[:, 3 * w:3 * w + LRU_WIDTH])

    def proj_gate():
        staged["gate"] = _dot(h, w_ref[:, 3 * w + LRU_WIDTH:])

    def proj_vt(c):
        toks = slice(c * KV_TILE, (c + 1) * KV_TILE)
        vt_ref[c] = _dot_nt(wvt_ref[...], h[toks]).astype(BF16)

    kv_tiles = TOKEN_TILE // KV_TILE
    vts = [functools.partial(proj_vt, c) for c in range(kv_tiles)]
    half = (kv_tiles + 1) // 2
    matmuls = [proj_q] + vts[:half] + [proj_k] + vts[half:] + [proj_xb, proj_gate]
    n_slabs = TOKEN_TILE // LRU_ROWS
    state = hstate_ref[...]
    for slab in range(n_slabs):
        xc, gates = _lru_conv_gates(slab, *conv_refs)
        for piece in matmuls[slab * len(matmuls) // n_slabs:
                             (slab + 1) * len(matmuls) // n_slabs]:
            piece()
        _lru_rows(slab, xc, gates, *row_refs)
        state = _lru_serial(slab, state, *ser_refs)
    hstate_ref[...] = state
    xpad_ref[0:SUBLANES, :] = xpad_ref[TOKEN_TILE:TOKEN_TILE + SUBLANES, :]
    xpad_ref[SUBLANES:SUBLANES + TOKEN_TILE, :] = staged["xb"]
    gate_ref[...] = staged["gate"]


def _ab_in(x2, seq_len, g, w_in, e, conv_w, conv_b, w_gates, ba, bx, lam):
    t = x2.shape[0]
    c = LRU_WIDTH
    assert seq_len % TOKEN_TILE == 0
    n_tiles = t // TOKEN_TILE
    row = lambda i: (jnp.minimum(i, n_tiles - 1), 0)
    prev_row = lambda i: (jnp.maximum(i - 1, 0), 0)
    kv_per_tile = TOKEN_TILE // KV_TILE
    return pl.pallas_call(
        functools.partial(_ab_in_kernel, seq_len // TOKEN_TILE),
        grid=(n_tiles + 1,),
        in_specs=[pl.BlockSpec((TOKEN_TILE, D_MODEL), row),
                  _resident((1, D_MODEL)),
                  _resident_layer(w_in, e),
                  _resident((CONV_WIDTH, c)), _resident((1, c)),
                  _resident((c, 2 * c)), _resident((1, c)), _resident((1, c)),
                  _resident((1, c))],
        out_specs=[pl.BlockSpec((TOKEN_TILE, DIFF_WIDTH), row)] * 2
        + [pl.BlockSpec((kv_per_tile, DIFF_WIDTH, KV_TILE), lambda i: row(i) + (0,))]
        + [pl.BlockSpec((TOKEN_TILE, c), prev_row)],
        out_shape=[jax.ShapeDtypeStruct((t, DIFF_WIDTH), BF16)] * 2
        + [jax.ShapeDtypeStruct((t // KV_TILE, DIFF_WIDTH, KV_TILE), BF16)]
        + [jax.ShapeDtypeStruct((t, c), BF16)],
        scratch_shapes=[pltpu.VMEM((DIFF_WIDTH, D_MODEL), BF16),
                        pltpu.VMEM((TOKEN_TILE + SUBLANES, c), F32),
                        pltpu.VMEM((TOKEN_TILE, c), F32),
                        pltpu.VMEM((SUBLANES, c), F32)]
        + [pltpu.VMEM((TOKEN_TILE // SUBLANES, SUBLANES, c), F32)] * 2,
        compiler_params=pltpu.CompilerParams(
            dimension_semantics=("arbitrary",), vmem_limit_bytes=VMEM_LIMIT_BYTES),
        name="ab_in_lru",
    )(x2, g, w_in, conv_w, conv_b, w_gates, ba, bx, lam)


def _attn_kernel(lam_init, q_ref, k_ref, vt_ref, lq1_ref, lk1_ref, lq2_ref, lk2_ref,
                 sg_ref, o_ref, m_ref, acc_ref, s_ref):
    first_q = pl.program_id(1) * Q_GROUPS
    d = DIFF_HEAD_DIM
    vd = DIFF_V_DIM
    per_tile = 2 * DIFF_HEADS
    n_chains = Q_GROUPS * per_tile

    m_ref[...] = jnp.full(m_ref.shape, NEG_BIG, F32)
    acc_ref[...] = jnp.zeros(acc_ref.shape, F32)
    ones = jnp.ones((BF16_SUBLANES, KV_TILE), BF16)

    lane = lax.broadcasted_iota(jnp.int32, (Q_TILE, vd), 1)
    qz = []
    for c in range(n_chains):
        t, h, mp = c // per_tile, (c % per_tile) // 2, c % 2
        qh = q_ref[t * Q_TILE:(t + 1) * Q_TILE, h * vd:(h + 1) * vd]
        keep = (lane >= d) if mp else (lane < d)
        qz.append(jnp.where(keep, qh, jnp.zeros_like(qh)))

    def head_of(c):
        return (c % per_tile) // 2

    def scores(c, j):
        h = head_of(c)
        start = pl.multiple_of(j * KV_TILE, KV_TILE)
        kb = k_ref[pl.ds(start, KV_TILE), h * vd:(h + 1) * vd]
        return _dot_nt(kb, qz[c])

    def softmax_update(c, s):
        m_prev = m_ref[c]
        m_new = jnp.maximum(m_prev, jnp.max(s, axis=0, keepdims=True))
        alpha = jnp.exp2(m_prev - m_new)
        p = jnp.exp2(s - m_new)
        m_ref[c] = m_new
        return p.astype(BF16), alpha

    def accumulate(c, j, p, alpha):
        h = head_of(c)
        vb = vt_ref[j, h * vd:(h + 1) * vd, :]
        acc_ref[c] = alpha * acc_ref[c] + _dot(jnp.concatenate([vb, ones], axis=0), p)

    def step(j, chains, masked, prefetch):
        pending = None
        for c in chains:
            s = s_ref[c]
            if c in prefetch:
                s_ref[c] = scores(c, j + 1)
            if c in masked:
                s = jnp.where(diagonal, s, NEG_BIG)
            p, alpha = softmax_update(c, s)
            if pending is not None:
                accumulate(*pending)
            pending = (c, j, p, alpha)
        accumulate(*pending)

    everything = range(n_chains)
    for c in everything:
        s_ref[c] = scores(c, 0)

    def body(j, carry):
        step(j, everything, (), everything)
        return carry

    lax.fori_loop(0, first_q, body, 0)
    kc = lax.broadcasted_iota(jnp.int32, (KV_TILE, Q_TILE), 0) // CHUNK
    qc = lax.broadcasted_iota(jnp.int32, (KV_TILE, Q_TILE), 1) // CHUNK
    diagonal = kc <= qc
    for t in range(Q_GROUPS):
        tile = range(t * per_tile, (t + 1) * per_tile)
        later = range((t + 1) * per_tile, n_chains)
        step(first_q + t, range(t * per_tile, n_chains), tile, later)

    lam = (jnp.exp(jnp.sum(lq1_ref[...] * lk1_ref[...], axis=-1, keepdims=True))
           - jnp.exp(jnp.sum(lq2_ref[...] * lk2_ref[...], axis=-1, keepdims=True))
           + lam_init)
    def normalized(c):
        return acc_ref[c, 0:vd, :] * (1.0 / acc_ref[c, vd:vd + 1, :])

    for c in range(0, n_chains, 2):
        t, h = c // per_tile, head_of(c)
        o = normalized(c) - lam * normalized(c + 1)
        o = o * lax.rsqrt(jnp.mean(o * o, axis=0, keepdims=True) + EPS)
        o = o * (sg_ref[...] * (1.0 - lam_init))
        o_ref[t * Q_TILE:(t + 1) * Q_TILE, h * vd:(h + 1) * vd] = o.T.astype(o_ref.dtype)


def _diff_attention(q, k, vt, lq1, lk1, lq2, lk2, subln_col, lam_init):
    b, s, w = q.shape
    q_rows = Q_GROUPS * Q_TILE
    assert Q_TILE == KV_TILE and Q_TILE % CHUNK == 0 and s % q_rows == 0
    vd = DIFF_V_DIM
    n_maps = Q_GROUPS * 2 * DIFF_HEADS
    vec = lambda n: _resident((1, n))
    return pl.pallas_call(
        functools.partial(_attn_kernel, lam_init),
        grid=(b, s // q_rows),
        in_specs=[pl.BlockSpec((None, q_rows, w), lambda bi, i: (bi, i, 0)),
                  pl.BlockSpec((None, s, w), lambda bi, i: (bi, 0, 0)),
                  pl.BlockSpec((s // KV_TILE, w, KV_TILE), lambda bi, i: (bi, 0, 0)),
                  vec(DIFF_HEAD_DIM), vec(DIFF_HEAD_DIM), vec(DIFF_HEAD_DIM),
                  vec(DIFF_HEAD_DIM), _resident((vd, 1))],
        out_specs=pl.BlockSpec((None, q_rows, w), lambda bi, i: (bi, i, 0)),
        out_shape=jax.ShapeDtypeStruct((b, s, w), BF16),
        scratch_shapes=[pltpu.VMEM((n_maps, 1, Q_TILE), F32),
                        pltpu.VMEM((n_maps, vd + BF16_SUBLANES, Q_TILE), F32),
                        pltpu.VMEM((n_maps, KV_TILE, Q_TILE), F32)],
        compiler_params=pltpu.CompilerParams(
            dimension_semantics=("arbitrary",) * 2, vmem_limit_bytes=VMEM_LIMIT_BYTES),
        name="diff_attention",
    )(q, k, vt, lq1, lk1, lq2, lk2, subln_col)


def _lru_conv_gates(slab, cw_ref, cb_ref, wg_ref, xpad_ref):
    rows = LRU_ROWS
    groups = rows // SUBLANES
    xe = xpad_ref[slab * rows:slab * rows + rows + SUBLANES, :]
    xe = xe.reshape(groups + 1, SUBLANES, LRU_WIDTH)
    sub = lax.broadcasted_iota(jnp.int32, (groups, SUBLANES, LRU_WIDTH), 1)
    xc = cb_ref[...] + cw_ref[CONV_WIDTH - 1:CONV_WIDTH, :] * xe[1:]
    for back in range(1, CONV_WIDTH):
        rolled = pltpu.roll(xe, back, 1)
        shifted = jnp.where(sub >= back, rolled[1:], rolled[:-1])
        tap = CONV_WIDTH - 1 - back
        xc = xc + cw_ref[tap:tap + 1, :] * shifted
    xc = xc.reshape(rows, LRU_WIDTH)
    return xc, _dot(xc.astype(BF16), wg_ref[...])


def _lru_rows(slab, xc, gates, ba_ref, bx_ref, lam_ref, aloc_ref, bloc_ref):
    ta = jnp.tanh(gates[:, :LRU_WIDTH] + ba_ref[...])
    ig = 0.5 * jnp.tanh(gates[:, LRU_WIDTH:] + bx_ref[...]) + 0.5
    z = -lam_ref[...]
    softplus = jnp.maximum(z, 0.0) + jnp.log1p(jnp.exp(-jnp.abs(z)))
    log_a = (-0.5 * LRU_C * softplus) * (ta + 1.0)
    a = jnp.exp(log_a)
    bt = jnp.sqrt(-jnp.tanh(log_a) * (a * a + 1.0)) * (ig * xc)

    groups = LRU_ROWS // SUBLANES
    a3 = a.reshape(groups, SUBLANES, LRU_WIDTH)
    b3 = bt.reshape(groups, SUBLANES, LRU_WIDTH)
    sub = lax.broadcasted_iota(jnp.int32, a3.shape, 1)
    shift = 1
    while shift < SUBLANES:
        keep = sub >= shift
        a_prev = jnp.where(keep, pltpu.roll(a3, shift, 1), 1.0)
        b_prev = jnp.where(keep, pltpu.roll(b3, shift, 1), 0.0)
        b3 = a3 * b_prev + b3
        a3 = a3 * a_prev
        shift *= 2
    aloc_ref[slab * groups:(slab + 1) * groups] = a3
    bloc_ref[slab * groups:(slab + 1) * groups] = b3


def _lru_serial(slab, h_in, gate_ref, aloc_ref, bloc_ref, yb_ref):
    groups = LRU_ROWS // SUBLANES
    hs = []
    for g in range(slab * groups, (slab + 1) * groups):
        hg = aloc_ref[g] * h_in + bloc_ref[g]
        hs.append(hg)
        h_in = jnp.broadcast_to(hg[SUBLANES - 1:SUBLANES, :], hg.shape)
    rows = slice(slab * LRU_ROWS, (slab + 1) * LRU_ROWS)
    yb = jax.nn.gelu(gate_ref[rows, :]) * jnp.concatenate(hs, axis=0)
    yb_ref[rows, :] = yb.astype(yb_ref.dtype)
    return h_in


def _block_diag(w):
    nb, d, _ = w.shape
    eye = jnp.eye(nb, dtype=w.dtype)
    return (eye[:, None, :, None] * w[:, :, None, :]).reshape(nb * d, nb * d)


def _ffn(x, g, w1_ref, w2_ref):
    h = _rms(x, g).astype(BF16)
    acc = None
    for c in range(D_FF // FF_CHUNK):
        cols = slice(c * FF_CHUNK, (c + 1) * FF_CHUNK)
        t = jnp.maximum(_dot(h, w1_ref[:, cols]), 0.0)
        part = _dot((t * t).astype(BF16), w2_ref[cols, :])
        acc = part if acc is None else acc + part
    return x + acc


def _finish(x, gfin_ref, o_ref):
    if gfin_ref is not None:
        x = _rms(x, gfin_ref[...])
    o_ref[...] = x


def _ab_out_ffn_kernel(final, x_ref, ya_ref, yb_ref, wo_ref, gf_ref, w1_ref, w2_ref,
                       *rest):
    gfin_ref, o_ref = rest if final else (None, rest[0])
    y = jnp.concatenate([ya_ref[...], yb_ref[...]], axis=1)
    x = x_ref[...] + _dot(y, wo_ref[...])
    _finish(_ffn(x, gf_ref[...], w1_ref, w2_ref), gfin_ref, o_ref)


def _ab_out_ffn(x2, ya, yb, w_out, e, g_ffn, w1, w2, layer, g_final):
    t = x2.shape[0]
    row = lambda i: (i, 0)
    final = g_final is not None
    in_specs = [pl.BlockSpec((MLP_TILE, D_MODEL), row),
                pl.BlockSpec((MLP_TILE, DIFF_WIDTH), row),
                pl.BlockSpec((MLP_TILE, LRU_WIDTH), row),
                _resident_layer(w_out, e), _resident((1, D_MODEL)),
                _resident_layer(w1, layer), _resident_layer(w2, layer)]
    args = [x2, ya, yb, w_out, g_ffn, w1, w2]
    if final:
        in_specs.append(_resident((1, D_MODEL)))
        args.append(g_final)
    return pl.pallas_call(
        functools.partial(_ab_out_ffn_kernel, final),
        grid=(t // MLP_TILE,),
        in_specs=in_specs,
        out_specs=pl.BlockSpec((MLP_TILE, D_MODEL), row),
        out_shape=jax.ShapeDtypeStruct((t, D_MODEL), F32),
        compiler_params=pltpu.CompilerParams(
            dimension_semantics=("arbitrary",), vmem_limit_bytes=VMEM_LIMIT_BYTES),
        name="ab_out_ffn",
    )(*args)


def _sgu_ffn_kernel(final, x_ref, gm_ref, win_ref, lng_ref, lnb_ref, ws_ref, bs_ref,
                    wout_ref, gf_ref, w1_ref, w2_ref, *rest):
    if final:
        gfin_ref, o_ref, gated_ref = rest
    else:
        gfin_ref, (o_ref, gated_ref) = None, rest
    x = x_ref[...]
    h = _rms(x, gm_ref[...]).astype(BF16)
    u_cols = slice(0, SGU_WIDTH)
    v_cols = slice(SGU_WIDTH, 2 * SGU_WIDTH)
    v = jax.nn.gelu(_dot(h, win_ref[:, v_cols]))
    mu = jnp.mean(v, axis=-1, keepdims=True)
    vc = v - mu
    var = jnp.mean(vc * vc, axis=-1, keepdims=True)
    vn = (vc * lax.rsqrt(var + EPS) * lng_ref[...] + lnb_ref[...]).astype(BF16)
    u = jax.nn.gelu(_dot(h, win_ref[:, u_cols]))

    pc = lax.broadcasted_iota(jnp.int32, (SGU_BLOCK, SGU_BLOCK), 0) // CHUNK
    qc = lax.broadcasted_iota(jnp.int32, (SGU_BLOCK, SGU_BLOCK), 1) // CHUNK
    mask = qc <= pc
    for g in range(SGU_GROUPS):
        cols = slice(g * SGU_GROUP_DIM, (g + 1) * SGU_GROUP_DIM)
        w = jnp.where(mask, ws_ref[g], 0.0).astype(BF16)
        bias = bs_ref[:, g:g + 1]
        for n in range(SGU_TILE // SGU_BLOCK):
            rows = slice(n * SGU_BLOCK, (n + 1) * SGU_BLOCK)
            mixed = _dot(w, vn[rows, cols]) + bias
            gated_ref[rows, cols] = (u[rows, cols] * mixed).astype(BF16)

    x = x + _dot(gated_ref[...], wout_ref[...])
    _finish(_ffn(x, gf_ref[...], w1_ref, w2_ref), gfin_ref, o_ref)


def _sgu_ffn(x2, g_mix, w_in, ln_g, ln_b, w_s, b_s_t, w_out, o, g_ffn, w1, w2, layer,
             g_final):
    t = x2.shape[0]
    row = lambda i: (i, 0)
    final = g_final is not None
    in_specs = [pl.BlockSpec((SGU_TILE, D_MODEL), row),
                _resident((1, D_MODEL)), _resident_layer(w_in, o),
                _resident((1, SGU_WIDTH)), _resident((1, SGU_WIDTH)),
                _resident_layer(w_s, o), _resident(b_s_t.shape),
                _resident_layer(w_out, o), _resident((1, D_MODEL)),
                _resident_layer(w1, layer), _resident_layer(w2, layer)]
    args = [x2, g_mix, w_in, ln_g, ln_b, w_s, b_s_t, w_out, g_ffn, w1, w2]
    if final:
        in_specs.append(_resident((1, D_MODEL)))
        args.append(g_final)
    return pl.pallas_call(
        functools.partial(_sgu_ffn_kernel, final),
        grid=(t // SGU_TILE,),
        in_specs=in_specs,
        out_specs=pl.BlockSpec((SGU_TILE, D_MODEL), row),
        out_shape=jax.ShapeDtypeStruct((t, D_MODEL), F32),
        scratch_shapes=[pltpu.VMEM((SGU_TILE, SGU_WIDTH), BF16)],
        compiler_params=pltpu.CompilerParams(
            dimension_semantics=("arbitrary",), vmem_limit_bytes=VMEM_LIMIT_BYTES),
        name="sgu_ffn",
    )(*args)


def kernel(x, norm_mix, norm_ffn, norm_final, ab_w_in, diff_lq1, diff_lk1, diff_lq2,
           diff_lk2, diff_subln, lru_conv_w, lru_conv_b, lru_wa, lru_ba, lru_wx, lru_bx,
           lru_lambda, ab_w_out, c_w_in, c_ln_g, c_ln_b, c_w_s, c_b_s, c_w_out, ffn_w1,
           ffn_w2):
    b, s, d = x.shape
    t = b * s
    assert t % TOKEN_TILE == 0 and TOKEN_TILE % SGU_BLOCK == 0
    x2 = x.reshape(t, d)
    vec = lambda a: a.reshape(1, -1)
    w1, w2 = ffn_w1.astype(BF16), ffn_w2.astype(BF16)
    w_in_ab, w_out_ab = ab_w_in.astype(BF16), ab_w_out.astype(BF16)
    w_in_c, w_out_c = c_w_in.astype(BF16), c_w_out.astype(BF16)
    for layer in range(DEPTH):
        g_final = vec(norm_final) if layer == DEPTH - 1 else None
        g_ffn = vec(norm_ffn[layer])
        if layer % 2 == 0:
            e = layer // 2
            lam_init = 0.8 - 0.6 * math.exp(-0.3 * layer)
            w_gates = jnp.concatenate(
                [_block_diag(lru_wa[e]), _block_diag(lru_wx[e])], axis=1).astype(BF16) * 0.5
            q, k, vt, yb = _ab_in(
                x2, s, vec(norm_mix[layer]), w_in_ab, e, lru_conv_w[e], vec(lru_conv_b[e]),
                w_gates, vec(lru_ba[e]) * 0.5, vec(lru_bx[e]) * 0.5, vec(lru_lambda[e]))
            seq = lambda a: a.reshape(b, s, a.shape[-1])
            ya = _diff_attention(seq(q), seq(k), vt, vec(diff_lq1[e]), vec(diff_lk1[e]),
                                 vec(diff_lq2[e]), vec(diff_lk2[e]),
                                 diff_subln[e].reshape(-1, 1), lam_init)
            x2 = _ab_out_ffn(x2, ya.reshape(t, -1), yb, w_out_ab, e, g_ffn, w1, w2, layer,
                             g_final)
        else:
            o = layer // 2
            x2 = _sgu_ffn(x2, vec(norm_mix[layer]), w_in_c, vec(c_ln_g[o]), vec(c_ln_b[o]),
                          c_w_s, c_b_s[o].T, w_out_c, o, g_ffn, w1, w2, layer, g_final)
    return x2.reshape(b, s, d)
```
